```python
import math
import jax, jax.numpy as jnp
from jax import lax
import numpy as np

D_MODEL = 2048
BATCH = 4
SEQ = 4096
DEPTH = 4

N_MEM = 256
N_MIXERS = 3
BRANCH_WIDTH = D_MODEL
MEM_HEADS = 4
MEM_HEAD_DIM = 128
MEM_WIDTH = MEM_HEADS * MEM_HEAD_DIM
MIX_WIDTH = BRANCH_WIDTH - MEM_WIDTH

SWA_HEAD_DIM = 64
SWA_Q_HEADS = MIX_WIDTH // SWA_HEAD_DIM
SWA_KV_HEADS = SWA_Q_HEADS // 8
SWA_WINDOW = 128

MOBA_HEAD_DIM = 128
MOBA_HEADS = MIX_WIDTH // MOBA_HEAD_DIM
MOBA_BLOCK = 256
MOBA_TOPK = 3
MOBA_Q_CHUNK = 32

RET_HEADS = 6
RET_V_DIM = MIX_WIDTH // RET_HEADS
RET_QK_DIM = RET_V_DIM // 2
RET_CHUNK = 128
RET_THETA = 10000.0

ROPE_THETA = 500000.0
ROPE_FRACTION = 4
EPS = 1e-6

IN_COLS = (
    SWA_Q_HEADS * SWA_HEAD_DIM + 2 * SWA_KV_HEADS * SWA_HEAD_DIM + MEM_WIDTH + BRANCH_WIDTH,
    3 * MIX_WIDTH + MEM_WIDTH + BRANCH_WIDTH,
    2 * RET_HEADS * RET_QK_DIM + RET_HEADS * RET_V_DIM + MEM_WIDTH + BRANCH_WIDTH,
)

kernel_name = "hybrid_swa_moba_retention_trunk"

F32 = jnp.float32


def rmsnorm(x, g):
    xf = x.astype(F32)
    y = xf * lax.rsqrt(jnp.mean(xf * xf, axis=-1, keepdims=True) + EPS)
    return (y * g.astype(F32)).astype(x.dtype)


def rope_angles(positions, rot_dim, theta):
    inv = theta ** (-jnp.arange(0, rot_dim, 2, dtype=F32) / rot_dim)
    ang = positions.astype(F32)[..., None] * inv
    return jnp.cos(ang)[:, :, None, :], jnp.sin(ang)[:, :, None, :]


def apply_rope(x, cos, sin):
    r = 2 * cos.shape[-1]
    xf = x[..., :r].astype(F32)
    x1, x2 = xf[..., : r // 2], xf[..., r // 2:]
    rot = jnp.concatenate([x1 * cos - x2 * sin, x2 * cos + x1 * sin], axis=-1).astype(x.dtype)
    return jnp.concatenate([rot, x[..., r:]], axis=-1)


def swa_sink_attention(q, k, v, sinks):
    B, S, Hq, dh = q.shape
    Hkv = k.shape[2]
    G = Hq // Hkv
    W = SWA_WINDOW
    nb = S // W
    qb = q.reshape(B, nb, W, Hkv, G, dh)

    def with_prev(t):
        tb = t.reshape(B, nb, W, Hkv, dh)
        prev = jnp.pad(tb, ((0, 0), (1, 0), (0, 0), (0, 0), (0, 0)))[:, :-1]
        return jnp.concatenate([prev, tb], axis=2)

    kb, vb = with_prev(k), with_prev(v)
    s = jnp.einsum('bnqhgd,bnkhd->bnhgqk', qb, kb, preferred_element_type=F32) * (dh ** -0.5)
    qi = jnp.arange(W)[:, None] + W
    ki = jnp.arange(2 * W)[None, :]
    blk = jnp.arange(nb)[:, None, None]
    valid = (ki <= qi) & (ki > qi - W) & ((blk > 0) | (ki >= W))
    s = jnp.where(valid[None, :, None, None], s, -jnp.inf)
    sink = sinks.astype(F32).reshape(1, 1, Hkv, G, 1, 1)
    m = jnp.maximum(jnp.max(s, axis=-1, keepdims=True), sink)
    p = jnp.exp(s - m)
    p = p / (jnp.sum(p, axis=-1, keepdims=True) + jnp.exp(sink - m))
    o = jnp.einsum('bnhgqk,bnkhd->bnqhgd', p.astype(v.dtype), vb)
    return o.reshape(B, S, Hq * dh)


def moba_attention(q, k, v):
    B, S, H, dh = q.shape
    L = MOBA_BLOCK
    C = MOBA_Q_CHUNK
    Sp = -(-S // L) * L
    nblk = Sp // L
    pad = ((0, 0), (0, Sp - S), (0, 0), (0, 0))
    qh = jnp.pad(q, pad).transpose(0, 2, 1, 3)
    kb = jnp.pad(k, pad).transpose(0, 2, 1, 3).reshape(B, H, nblk, L, dh)
    vb = jnp.pad(v, pad).transpose(0, 2, 1, 3).reshape(B, H, nblk, L, dh)
    scale = dh ** -0.5
    q_blk = jnp.arange(Sp) // L
    n_sel = min(MOBA_TOPK, nblk - 1)
    if n_sel > 0:
        kmean = jnp.mean(kb.astype(F32), axis=3)
        gate = jnp.einsum('bhsd,bhnd->bhsn', qh.astype(F32), kmean)
        past = jnp.arange(nblk)[None, :] < q_blk[:, None]
        gate = jnp.where(past, gate, -jnp.inf)
        _, sel = lax.top_k(gate, n_sel)
        sel_valid = sel < q_blk[:, None]
        bi = jnp.arange(B)[:, None, None, None]
        hi = jnp.arange(H)[None, :, None, None]

    def chunk_fn(c):
        start = c * C
        qc = lax.dynamic_slice_in_dim(qh, start, C, axis=2)
        own = start // L
        k_own = lax.dynamic_index_in_dim(kb, own, axis=2, keepdims=False)
        v_own = lax.dynamic_index_in_dim(vb, own, axis=2, keepdims=False)
        pos = start + jnp.arange(C)
        own_mask = (own * L + jnp.arange(L))[None, :] <= pos[:, None]
        s_own = jnp.einsum('bhqd,bhkd->bhqk', qc, k_own, preferred_element_type=F32) * scale
        s_own = jnp.where(own_mask, s_own, -jnp.inf)
        if n_sel > 0:
            sel_c = lax.dynamic_slice_in_dim(sel, start, C, axis=2)
            val_c = lax.dynamic_slice_in_dim(sel_valid, start, C, axis=2)
            k_sel = kb[bi, hi, sel_c]
            v_sel = vb[bi, hi, sel_c]
            s_sel = jnp.einsum('bhqd,bhqnkd->bhqnk', qc, k_sel, preferred_element_type=F32) * scale
            s_sel = jnp.where(val_c[..., None], s_sel, -jnp.inf).reshape(B, H, C, n_sel * L)
            p = jax.nn.softmax(jnp.concatenate([s_sel, s_own], axis=-1), axis=-1).astype(v.dtype)
            p_sel = p[..., : n_sel * L].reshape(B, H, C, n_sel, L)
            o = (jnp.einsum('bhqnk,bhqnkd->bhqd', p_sel, v_sel)
                 + jnp.einsum('bhqk,bhkd->bhqd', p[..., n_sel * L:], v_own))
        else:
            p = jax.nn.softmax(s_own, axis=-1).astype(v.dtype)
            o = jnp.einsum('bhqk,bhkd->bhqd', p, v_own)
        return o

    out = lax.map(chunk_fn, jnp.arange(Sp // C))
    out = out.transpose(1, 0, 3, 2, 4).reshape(B, Sp, H * dh)
    return out[:, :S]


def retention(q, k, v):
    B, S, H, dk = q.shape
    dv = v.shape[-1]
    T = RET_CHUNK
    nc = S // T
    log_g = jnp.log1p(-jnp.exp(jnp.linspace(math.log(1.0 / 32), math.log(1.0 / 512), H, dtype=F32)))
    k = k * (dk ** -0.5)
    qc = q.reshape(B, nc, T, H, dk)
    kc = k.reshape(B, nc, T, H, dk)
    vc = v.reshape(B, nc, T, H, dv)
    i = jnp.arange(T, dtype=F32)
    diff = i[:, None] - i[None, :]
    decay = jnp.where(diff >= 0, jnp.exp(jnp.maximum(diff, 0.0)[None] * log_g[:, None, None]), 0.0)
    s = jnp.einsum('bnqhd,bnkhd->bnhqk', qc, kc, preferred_element_type=F32) * decay[None, None]
    inner = jnp.einsum('bnhqk,bnkhe->bnqhe', s.astype(v.dtype), vc).astype(F32)
    zeta = jnp.exp((T - 1 - i)[:, None] * log_g[None, :])
    kv = jnp.einsum('bnkhd,bnkhe->bnhde', kc.astype(F32) * zeta[:, :, None], vc.astype(F32))
    g_chunk = jnp.exp(T * log_g)[None, :, None, None]

    def step(R, kv_n):
        return g_chunk * R + kv_n, R

    _, R_prev = lax.scan(step, jnp.zeros((B, H, dk, dv), F32), kv.transpose(1, 0, 2, 3, 4))
    xi = jnp.exp((i + 1)[:, None] * log_g[None, :])
    cross = jnp.einsum('bnqhd,nbhde->bnqhe', qc.astype(F32) * xi[:, :, None], R_prev)
    o = inner + cross
    o = o * lax.rsqrt(jnp.mean(o * o, axis=-1, keepdims=True) + EPS)
    return o.reshape(B, S, H * dv).astype(v.dtype)


def memory_attention(qm, mem_k, mem_v):
    B, S, Hm, dm = qm.shape
    s = jnp.einsum('bshd,bnhd->bhsn', qm, mem_k, preferred_element_type=F32) * (dm ** -0.5)
    p = jax.nn.softmax(s, axis=-1).astype(mem_v.dtype)
    return jnp.einsum('bhsn,bnhd->bshd', p, mem_v).reshape(B, S, Hm * dm)


def hybrid_layer(x, mixer_id, g, w_in, w_out, sinks, rope_a, rope_b, rope_c, mem_k, mem_v):
    B, S, _ = x.shape
    h = rmsnorm(x, g)
    proj = h @ w_in
    n_mix = w_in.shape[1] - MEM_WIDTH - BRANCH_WIDTH
    mix_in = proj[..., :n_mix]
    qm = proj[..., n_mix:n_mix + MEM_WIDTH].reshape(B, S, MEM_HEADS, MEM_HEAD_DIM)
    z = proj[..., n_mix + MEM_WIDTH:]
    if mixer_id == 0:
        nq = SWA_Q_HEADS * SWA_HEAD_DIM
        nkv = SWA_KV_HEADS * SWA_HEAD_DIM
        q = mix_in[..., :nq].reshape(B, S, SWA_Q_HEADS, SWA_HEAD_DIM)
        k = mix_in[..., nq:nq + nkv].reshape(B, S, SWA_KV_HEADS, SWA_HEAD_DIM)
        v = mix_in[..., nq + nkv:].reshape(B, S, SWA_KV_HEADS, SWA_HEAD_DIM)
        q, k = apply_rope(q, *rope_a), apply_rope(k, *rope_a)
        mix_out = swa_sink_attention(q, k, v, sinks)
    elif mixer_id == 1:
        q, k, v = [t.reshape(B, S, MOBA_HEADS, MOBA_HEAD_DIM) for t in jnp.split(mix_in, 3, axis=-1)]
        q, k = apply_rope(q, *rope_b), apply_rope(k, *rope_b)
        mix_out = moba_attention(q, k, v)
    else:
        nqk = RET_HEADS * RET_QK_DIM
        q = mix_in[..., :nqk].reshape(B, S, RET_HEADS, RET_QK_DIM)
        k = mix_in[..., nqk:2 * nqk].reshape(B, S, RET_HEADS, RET_QK_DIM)
        v = mix_in[..., 2 * nqk:].reshape(B, S, RET_HEADS, RET_V_DIM)
        q, k = apply_rope(q, *rope_c), apply_rope(k, *rope_c)
        mix_out = retention(q, k, v)
    mem_out = memory_attention(qm, mem_k, mem_v)
    y = jnp.concatenate([mix_out, mem_out], axis=-1) * jax.nn.silu(z)
    return x + y @ w_out


def setup_inputs(seed: int = 0) -> dict:
    key = jax.random.key(seed)
    keys = jax.random.split(key, 4 + 4 * DEPTH + 1)
    out = {}
    out['x'] = jax.random.normal(keys[0], (BATCH, SEQ, D_MODEL), F32)
    out['mem'] = jax.random.normal(keys[1], (BATCH, N_MEM, D_MODEL), F32)
    out['positions'] = jnp.broadcast_to(jnp.arange(SEQ, dtype=jnp.int32), (BATCH, SEQ))
    out['mem_norm'] = 1.0 + 0.02 * jax.random.normal(keys[2], (D_MODEL,), F32)
    out['w_mem_kv'] = jax.random.normal(keys[3], (D_MODEL, 2 * MEM_WIDTH), F32) * D_MODEL ** -0.5
    for i in range(DEPTH):
        kg, ki, ks, ko = keys[4 + 4 * i: 8 + 4 * i]
        mid = i % N_MIXERS
        out[f'norm_{i}'] = 1.0 + 0.02 * jax.random.normal(kg, (D_MODEL,), F32)
        out[f'w_in_{i}'] = jax.random.normal(ki, (D_MODEL, IN_COLS[mid]), F32) * D_MODEL ** -0.5
        if mid == 0:
            out[f'sinks_{i}'] = jax.random.normal(ks, (SWA_Q_HEADS,), F32)
        out[f'w_out_{i}'] = jax.random.normal(ko, (BRANCH_WIDTH, D_MODEL), F32) * (0.5 * BRANCH_WIDTH ** -0.5)
    out['final_norm'] = 1.0 + 0.02 * jax.random.normal(keys[-1], (D_MODEL,), F32)
    return out


def reference(x, mem, positions, mem_norm, w_mem_kv,
              norm_0, w_in_0, sinks_0, w_out_0,
              norm_1, w_in_1, w_out_1,
              norm_2, w_in_2, w_out_2,
              norm_3, w_in_3, sinks_3, w_out_3,
              final_norm):
    B = x.shape[0]
    layers = [(norm_0, w_in_0, w_out_0, sinks_0),
              (norm_1, w_in_1, w_out_1, None),
              (norm_2, w_in_2, w_out_2, None),
              (norm_3, w_in_3, w_out_3, sinks_3)]
    mkv = rmsnorm(mem, mem_norm) @ w_mem_kv
    mem_k = mkv[..., :MEM_WIDTH].reshape(B, N_MEM, MEM_HEADS, MEM_HEAD_DIM)
    mem_v = mkv[..., MEM_WIDTH:].reshape(B, N_MEM, MEM_HEADS, MEM_HEAD_DIM)
    rope_a = rope_angles(positions, SWA_HEAD_DIM // ROPE_FRACTION, ROPE_THETA)
    rope_b = rope_angles(positions, MOBA_HEAD_DIM // ROPE_FRACTION, ROPE_THETA)
    rope_c = rope_angles(positions, RET_QK_DIM, RET_THETA)
    h = x
    for i in range(DEPTH):
        g, w_in, w_out, sinks = layers[i]
        h = hybrid_layer(h, i % N_MIXERS, g, w_in, w_out, sinks, rope_a, rope_b, rope_c, mem_k, mem_v)
    return rmsnorm(h, final_norm)
```

```python
import functools
import math

import jax
import jax.numpy as jnp
from jax import lax
from jax.experimental import pallas as pl
from jax.experimental.pallas import tpu as pltpu

F32 = jnp.float32
BF16 = jnp.bfloat16

D_MODEL = 2048
N_MEM = 256
N_MIXERS = 3
BRANCH_WIDTH = D_MODEL
MEM_HEADS = 4
MEM_HEAD_DIM = 128
MEM_WIDTH = MEM_HEADS * MEM_HEAD_DIM
MIX_WIDTH = BRANCH_WIDTH - MEM_WIDTH

SWA_HEAD_DIM = 64
SWA_Q_HEADS = MIX_WIDTH // SWA_HEAD_DIM
SWA_KV_HEADS = SWA_Q_HEADS // 8
SWA_WINDOW = 128

MOBA_HEAD_DIM = 128
MOBA_HEADS = MIX_WIDTH // MOBA_HEAD_DIM
MOBA_BLOCK = 256
MOBA_TOPK = 3

RET_HEADS = 6
RET_V_DIM = MIX_WIDTH // RET_HEADS
RET_QK_DIM = RET_V_DIM // 2
RET_CHUNK = 128
RET_THETA = 10000.0

ROPE_THETA = 500000.0
ROPE_FRACTION = 4
EPS = 1e-6

LANES = 128
COL_TILE = 512
NEG = -1e30
VMEM_LIMIT = 56 * 1024 * 1024

_NT = (((1,), (1,)), ((), ()))
_TN = (((0,), (0,)), ((), ()))


def _cparams(n_axes):
    return pltpu.CompilerParams(dimension_semantics=("arbitrary",) * n_axes,
                                vmem_limit_bytes=VMEM_LIMIT)


def _silu(z):
    return z * (1.0 / (1.0 + jnp.exp(-z)))


def _rope_tables(positions, head_dim, rot_dim, theta):
    m = positions.size
    half = rot_dim // 2
    inv = theta ** (-jnp.arange(0, rot_dim, 2, dtype=F32) / rot_dim)
    ang = positions.astype(F32).reshape(m, 1) * inv
    cos, sin = jnp.cos(ang), jnp.sin(ang)
    rest = head_dim - rot_dim
    ones = jnp.ones((m, rest), F32)
    z_half = jnp.zeros((m, half), F32)
    z_rest = jnp.zeros((m, rest), F32)
    c = jnp.concatenate([cos, cos, ones], axis=1)
    s_plus = jnp.concatenate([z_half, sin, z_rest], axis=1)
    s_minus = jnp.concatenate([-sin, z_half, z_rest], axis=1)
    if 2 * half == LANES:
        s_plus, s_minus = s_plus + s_minus, jnp.zeros_like(s_minus)
    rep = LANES // head_dim
    tile = lambda t: jnp.tile(t, (1, rep)) if rep > 1 else t
    return tile(c), tile(s_plus), tile(s_minus)


def _in_proj_kernel(x_ref, g_ref, w_ref, c_ref, sp_ref, sm_ref, ob_ref, oz_ref, h_scr, *,
                    tile_cfgs, n_b, half, k_scale):
    j = pl.program_id(1)

    @pl.when(j == 0)
    def _():
        x = x_ref[...]
        ms = jnp.mean(x * x, axis=-1, keepdims=True)
        h_scr[...] = ((x * lax.rsqrt(ms + EPS)) * g_ref[...]).astype(BF16)

    acc = jnp.dot(h_scr[...], w_ref[...], preferred_element_type=F32)

    def rope(a):
        out = a * c_ref[...] + pltpu.roll(a, half, 1) * sp_ref[...]
        if 2 * half != LANES:
            out = out + pltpu.roll(a, LANES - half, 1) * sm_ref[...]
        return out

    for (lo, hi, kinds) in tile_cfgs:
        @pl.when((j >= lo) & (j < hi))
        def _(kinds=kinds):
            for gi, kind in enumerate(kinds):
                a = acc[:, gi * LANES:(gi + 1) * LANES]
                if kind in ("R", "RS"):
                    a = rope(a)
                if kind == "RS":
                    a = a * k_scale
                ob_ref[:, gi * LANES:(gi + 1) * LANES] = a.astype(BF16)

    @pl.when(j >= n_b)
    def _():
        oz_ref[...] = acc


def _in_proj(x2d, g, w_perm, kinds, tabs, half, k_scale, tm):
    m, d = x2d.shape
    n_tot = w_perm.shape[1]
    tn = COL_TILE
    gpt = tn // LANES
    n_t = n_tot // tn
    n_z = BRANCH_WIDTH // tn
    n_b = n_t - n_z
    cfgs = []
    for jt in range(n_b):
        kt = tuple(kinds[jt * gpt:(jt + 1) * gpt])
        if cfgs and cfgs[-1][2] == kt:
            cfgs[-1] = (cfgs[-1][0], jt + 1, kt)
        else:
            cfgs.append((jt, jt + 1, kt))
    kern = functools.partial(_in_proj_kernel, tile_cfgs=tuple(cfgs), n_b=n_b, half=half,
                             k_scale=k_scale)
    row = lambda i, j: (i, 0)
    return pl.pallas_call(
        kern,
        grid=(m // tm, n_t),
        in_specs=[
            pl.BlockSpec((tm, d), row),
            pl.BlockSpec((1, d), lambda i, j: (0, 0)),
            pl.BlockSpec((d, tn), lambda i, j: (0, j)),
            pl.BlockSpec((tm, LANES), row),
            pl.BlockSpec((tm, LANES), row),
            pl.BlockSpec((tm, LANES), row),
        ],
        out_specs=[
            pl.BlockSpec((tm, tn), lambda i, j: (i, jnp.minimum(j, n_b - 1))),
            pl.BlockSpec((tm, tn), lambda i, j: (i, jnp.maximum(j - n_b, 0))),
        ],
        out_shape=[
            jax.ShapeDtypeStruct((m, n_b * tn), BF16),
            jax.ShapeDtypeStruct((m, BRANCH_WIDTH), F32),
        ],
        scratch_shapes=[pltpu.VMEM((tm, d), BF16)],
        compiler_params=_cparams(2),
        name="in_proj",
    )(x2d, g.reshape(1, d), w_perm, *tabs)


def _mem_kv_kernel(x_ref, g_ref, w_ref, o_ref):
    x = x_ref[...]
    ms = jnp.mean(x * x, axis=-1, keepdims=True)
    h = ((x * lax.rsqrt(ms + EPS)) * g_ref[...]).astype(BF16)
    o_ref[...] = jnp.dot(h, w_ref[...], preferred_element_type=F32).astype(BF16)


def _mem_kv(mem2d, g, w_bf16):
    m, d = mem2d.shape
    n = w_bf16.shape[1]
    tm = min(m, 256)
    return pl.pallas_call(
        _mem_kv_kernel,
        grid=(m // tm,),
        in_specs=[
            pl.BlockSpec((tm, d), lambda i: (i, 0)),
            pl.BlockSpec((1, d), lambda i: (0, 0)),
            pl.BlockSpec((d, n), lambda i: (0, 0)),
        ],
        out_specs=pl.BlockSpec((tm, n), lambda i: (i, 0)),
        out_shape=jax.ShapeDtypeStruct((m, n), BF16),
        compiler_params=_cparams(1),
        name="mem_kv",
    )(mem2d, g.reshape(1, d), w_bf16)


def _swa_kernel(sink_ref, q_ref, kv_ref, o_ref, *, tq):
    w = SWA_WINDOW
    dh = SWA_HEAD_DIM
    g_per = SWA_Q_HEADS // SWA_KV_HEADS
    t = pl.program_id(1)
    start = t * tq
    kstart = pl.multiple_of(jnp.maximum(start - w, 0), w)
    win = tq + w
    qpos = start + lax.broadcasted_iota(jnp.int32, (tq, win), 0)
    kpos = kstart + lax.broadcasted_iota(jnp.int32, (tq, win), 1)
    valid = (kpos <= qpos) & (kpos > qpos - w)
    scale = dh ** -0.5
    v_off = 2 * LANES
    for h in range(SWA_KV_HEADS):
        k_h = kv_ref[pl.ds(kstart, win), h * dh:(h + 1) * dh]
        v_h = kv_ref[pl.ds(kstart, win), v_off + h * dh:v_off + (h + 1) * dh]
        for pair in range(g_per // 2):
            outs = []
            for sub in range(2):
                head = h * g_per + pair * 2 + sub
                q_g = q_ref[:, head * dh:(head + 1) * dh]
                s = lax.dot_general(q_g, k_h, _NT, preferred_element_type=F32) * scale
                s = jnp.where(valid, s, NEG)
                sink = sink_ref[head]
                mx = jnp.maximum(jnp.max(s, axis=-1, keepdims=True), sink)
                p = jnp.exp(s - mx)
                denom = jnp.sum(p, axis=-1, keepdims=True) + jnp.exp(sink - mx)
                o = jnp.dot(p.astype(BF16), v_h, preferred_element_type=F32)
                outs.append(o / denom)
            c0 = (h * g_per + pair * 2) * dh
            o_ref[:, c0:c0 + 2 * dh] = jnp.concatenate(outs, axis=1).astype(BF16)


def _swa(ob, sinks, batch, seq, tq):
    m = batch * seq
    nq = SWA_Q_HEADS * SWA_HEAD_DIM
    n_t = seq // tq
    kv_blk = nq // COL_TILE
    return pl.pallas_call(
        functools.partial(_swa_kernel, tq=tq),
        grid=(batch, n_t),
        in_specs=[
            pl.BlockSpec(memory_space=pltpu.SMEM),
            pl.BlockSpec((tq, nq), lambda b, t: (b * n_t + t, 0)),
            pl.BlockSpec((seq, COL_TILE), lambda b, t: (b, kv_blk)),
        ],
        out_specs=pl.BlockSpec((tq, nq), lambda b, t: (b * n_t + t, 0)),
        out_shape=jax.ShapeDtypeStruct((m, nq), BF16),
        compiler_params=_cparams(2),
        name="swa",
    )(sinks.astype(F32), ob, ob)


def _moba_kernel(q_ref, k_ref, v_ref, o_ref, kaug, kmean, *, seq):
    blk = MOBA_BLOCK
    dh = MOBA_HEAD_DIM
    nblk = seq // blk
    i = pl.program_id(2)

    @pl.when(i == 0)
    def _():
        k = k_ref[...]
        kaug[:, :dh] = k
        row_blk = lax.broadcasted_iota(jnp.int32, (seq, LANES), 0) // blk
        lane = lax.broadcasted_iota(jnp.int32, (seq, LANES), 1)
        kaug[:, dh:] = (row_blk == lane).astype(BF16)
        kmean[...] = jnp.zeros_like(kmean)
        kmean[:nblk, :] = jnp.mean(k.astype(F32).reshape(nblk, blk, dh), axis=1)

    q = q_ref[...]
    gate = lax.dot_general(q, kmean[...].astype(BF16), _NT, preferred_element_type=F32)
    col = lax.broadcasted_iota(jnp.int32, (blk, LANES), 1)
    colf = col.astype(F32)
    past = col < i
    g = jnp.where(past, gate, -jnp.inf)
    sel = col == i
    for _ in range(min(MOBA_TOPK, nblk - 1)):
        mx = jnp.max(g, axis=1, keepdims=True)
        idx = jnp.min(jnp.where(g == mx, colf, float(LANES)), axis=1, keepdims=True)
        hit = colf == idx
        sel = sel | (hit & past)
        g = jnp.where(hit, -jnp.inf, g)
    bias = jnp.where(sel, 0.0, NEG).astype(BF16)
    q_aug = jnp.concatenate([q, bias], axis=1)
    scale = dh ** -0.5

    r0 = pl.multiple_of(i * blk, blk)
    s = lax.dot_general(q_aug, kaug[pl.ds(r0, blk), :], _NT, preferred_element_type=F32) * scale
    causal = (lax.broadcasted_iota(jnp.int32, (blk, blk), 1)
              <= lax.broadcasted_iota(jnp.int32, (blk, blk), 0))
    s = jnp.where(causal, s, NEG)
    m0 = jnp.max(s, axis=1, keepdims=True)
    p = jnp.exp(s - m0)
    l0 = jnp.sum(p, axis=1, keepdims=True)
    acc0 = jnp.dot(p.astype(BF16), v_ref[pl.ds(r0, blk), :], preferred_element_type=F32)

    def body(jb, carry):
        m_prev, l_prev, acc = carry
        rj = pl.multiple_of(jb * blk, blk)
        sj = lax.dot_general(q_aug, kaug[pl.ds(rj, blk), :], _NT,
                             preferred_element_type=F32) * scale
        m_new = jnp.maximum(m_prev, jnp.max(sj, axis=1, keepdims=True))
        alpha = jnp.exp(m_prev - m_new)
        pj = jnp.exp(sj - m_new)
        l_new = alpha * l_prev + jnp.sum(pj, axis=1, keepdims=True)
        acc = alpha * acc + jnp.dot(pj.astype(BF16), v_ref[pl.ds(rj, blk), :],
                                    preferred_element_type=F32)
        return m_new, l_new, acc

    _, l_fin, acc = lax.fori_loop(0, i, body, (m0, l0, acc0))
    o_ref[...] = (acc / l_fin).astype(BF16)


def _moba(ob, batch, seq):
    m = batch * seq
    blk = MOBA_BLOCK
    dh = MOBA_HEAD_DIM
    nh = MOBA_HEADS
    nblk = seq // blk
    return pl.pallas_call(
        functools.partial(_moba_kernel, seq=seq),
        grid=(batch, nh, nblk),
        in_specs=[
            pl.BlockSpec((blk, dh), lambda b, h, i: (b * nblk + i, h)),
            pl.BlockSpec((seq, dh), lambda b, h, i: (b, nh + h)),
            pl.BlockSpec((seq, dh), lambda b, h, i: (b, 2 * nh + h)),
        ],
        out_specs=pl.BlockSpec((blk, dh), lambda b, h, i: (b * nblk + i, h)),
        out_shape=jax.ShapeDtypeStruct((m, nh * dh), BF16),
        scratch_shapes=[pltpu.VMEM((seq, 2 * dh), BF16), pltpu.VMEM((LANES, dh), F32)],
        compiler_params=_cparams(3),
        name="moba",
    )(ob, ob, ob)


def _ret_kernel(qk_ref, v_ref, dec_ref, xi_ref, zeta_ref, o_ref, r_scr, *, n_chunk, g_chunk):
    t = RET_CHUNK
    nh, dk, dv = RET_HEADS, RET_QK_DIM, RET_V_DIM

    @pl.when(pl.program_id(1) == 0)
    def _():
        r_scr[...] = jnp.zeros_like(r_scr)

    def body(c, carry):
        r0 = pl.multiple_of(c * t, t)
        for h in range(nh):
            q = qk_ref[pl.ds(r0, t), h * dk:(h + 1) * dk]
            k = qk_ref[pl.ds(r0, t), (nh + h) * dk:(nh + h + 1) * dk]
            v = v_ref[pl.ds(r0, t), h * dv:(h + 1) * dv]
            s = lax.dot_general(q, k, _NT, preferred_element_type=F32) * dec_ref[h]
            inner = jnp.dot(s.astype(BF16), v, preferred_element_type=F32)
            r_prev = r_scr[h]
            q_x = (q.astype(F32) * xi_ref[h]).astype(BF16)
            cross = jnp.dot(q_x, r_prev.astype(BF16), preferred_element_type=F32)
            o = inner + cross
            o = o * lax.rsqrt(jnp.mean(o * o, axis=-1, keepdims=True) + EPS)
            o_ref[pl.ds(r0, t), h * dv:(h + 1) * dv] = o.astype(BF16)
            k_z = (k.astype(F32) * zeta_ref[h]).astype(BF16)
            kv = lax.dot_general(k_z, v, _TN, preferred_element_type=F32)
            r_scr[h] = g_chunk[h] * r_prev + kv
        return carry

    lax.fori_loop(0, n_chunk, body, 0)


def _ret_constants():
    nh, t, dk = RET_HEADS, RET_CHUNK, RET_QK_DIM
    lin = [math.log(1.0 / 32) + (math.log(1.0 / 512) - math.log(1.0 / 32)) * h / (nh - 1)
           for h in range(nh)]
    log_g = [math.log1p(-math.exp(v)) for v in lin]
    i = jnp.arange(t, dtype=F32)
    lg = jnp.asarray(log_g, F32)
    diff = i[:, None] - i[None, :]
    decay = jnp.where(diff >= 0, jnp.exp(jnp.maximum(diff, 0.0)[None] * lg[:, None, None]), 0.0)
    xi = jnp.exp((i + 1)[None, :] * lg[:, None])
    zeta = jnp.exp((t - 1 - i)[None, :] * lg[:, None])
    xi_t = jnp.broadcast_to(xi[:, :, None], (nh, t, dk))
    zeta_t = jnp.broadcast_to(zeta[:, :, None], (nh, t, dk))
    g_chunk = tuple(math.exp(t * v) for v in log_g)
    return decay, xi_t, zeta_t, g_chunk


def _ret(ob, batch, seq, seg):
    m = batch * seq
    nh, dk, dv, t = RET_HEADS, RET_QK_DIM, RET_V_DIM, RET_CHUNK
    n_seg = seq // seg
    decay, xi_t, zeta_t, g_chunk = _ret_constants()
    wqk = 2 * nh * dk
    wv = nh * dv
    const = lambda b, s: (0, 0, 0)
    return pl.pallas_call(
        functools.partial(_ret_kernel, n_chunk=seg // t, g_chunk=g_chunk),
        grid=(batch, n_seg),
        in_specs=[
            pl.BlockSpec((seg, wqk), lambda b, s: (b * n_seg + s, 0)),
            pl.BlockSpec((seg, wv), lambda b, s: (b * n_seg + s, wqk // wv)),
            pl.BlockSpec((nh, t, t), const),
            pl.BlockSpec((nh, t, dk), const),
            pl.BlockSpec((nh, t, dk), const),
        ],
        out_specs=pl.BlockSpec((seg, wv), lambda b, s: (b * n_seg + s, 0)),
        out_shape=jax.ShapeDtypeStruct((m, wv), BF16),
        scratch_shapes=[pltpu.VMEM((nh, dk, dv), F32)],
        compiler_params=_cparams(2),
        name="retention",
    )(ob, ob, decay, xi_t, zeta_t)


def _out_kernel(mix_ref, qm_ref, z_ref, x_ref, mk_ref, mv_ref, w_ref, fn_ref, o_ref, y_scr, *,
                final):
    dm = MEM_HEAD_DIM
    scale = dm ** -0.5
    y_scr[:, :MIX_WIDTH] = (mix_ref[...].astype(F32) * _silu(z_ref[:, :MIX_WIDTH])).astype(BF16)
    for h in range(MEM_HEADS):
        lo, hi = h * dm, (h + 1) * dm
        s = lax.dot_general(qm_ref[:, lo:hi], mk_ref[:, lo:hi], _NT,
                            preferred_element_type=F32) * scale
        p = jnp.exp(s - jnp.max(s, axis=-1, keepdims=True))
        l = jnp.sum(p, axis=-1, keepdims=True)
        o = jnp.dot(p.astype(BF16), mv_ref[:, lo:hi], preferred_element_type=F32) / l
        y_scr[:, MIX_WIDTH + lo:MIX_WIDTH + hi] = (
            o * _silu(z_ref[:, MIX_WIDTH + lo:MIX_WIDTH + hi])).astype(BF16)
    out = x_ref[...] + jnp.dot(y_scr[...], w_ref[...], preferred_element_type=F32)
    if final:
        ms = jnp.mean(out * out, axis=-1, keepdims=True)
        out = (out * lax.rsqrt(ms + EPS)) * fn_ref[...]
    o_ref[...] = out


def _out_proj(mix, ob, qm_blk, z, x2d, mkv, w_out, final_norm, batch, seq, tm, final):
    m, d = x2d.shape
    per_b = seq // tm
    row = lambda i: (i, 0)
    return pl.pallas_call(
        functools.partial(_out_kernel, final=final),
        grid=(m // tm,),
        in_specs=[
            pl.BlockSpec((tm, MIX_WIDTH), row),
            pl.BlockSpec((tm, MEM_WIDTH), lambda i: (i, qm_blk)),
            pl.BlockSpec((tm, BRANCH_WIDTH), row),
            pl.BlockSpec((tm, d), row),
            pl.BlockSpec((N_MEM, MEM_WIDTH), lambda i: (i // per_b, 0)),
            pl.BlockSpec((N_MEM, MEM_WIDTH), lambda i: (i // per_b, 1)),
            pl.BlockSpec((BRANCH_WIDTH, d), lambda i: (0, 0)),
            pl.BlockSpec((1, d), lambda i: (0, 0)),
        ],
        out_specs=pl.BlockSpec((tm, d), row),
        out_shape=jax.ShapeDtypeStruct((m, d), F32),
        scratch_shapes=[pltpu.VMEM((tm, BRANCH_WIDTH), BF16)],
        compiler_params=_cparams(1),
        name="out_proj",
    )(mix, ob, z, x2d, mkv, mkv, w_out, final_norm.reshape(1, d))


def _permute_w_in(w_in, mixer):
    d = w_in.shape[0]
    n_mix = w_in.shape[1] - MEM_WIDTH - BRANCH_WIDTH
    w_qm = w_in[:, n_mix:n_mix + MEM_WIDTH]
    w_z = w_in[:, n_mix + MEM_WIDTH:]
    n_qm = MEM_WIDTH // LANES
    if mixer == 0:
        nq = SWA_Q_HEADS * SWA_HEAD_DIM
        nkv = SWA_KV_HEADS * SWA_HEAD_DIM
        pad = jnp.zeros((d, 2 * LANES - nkv), w_in.dtype)
        parts = [w_in[:, :nq], w_in[:, nq:nq + nkv], pad, w_in[:, nq + nkv:nq + 2 * nkv], pad]
        kinds = ["R"] * (nq // LANES) + ["R"] * 2 + ["P"] * 2
    elif mixer == 1:
        parts = [w_in[:, :n_mix]]
        kinds = ["R"] * (2 * MIX_WIDTH // LANES) + ["P"] * (MIX_WIDTH // LANES)
    else:
        nqk = RET_HEADS * RET_QK_DIM
        parts = [w_in[:, :n_mix]]
        kinds = ["R"] * (nqk // LANES) + ["RS"] * (nqk // LANES) + ["P"] * (MIX_WIDTH // LANES)
    w_perm = jnp.concatenate(parts + [w_qm, w_z], axis=1).astype(BF16)
    kinds = kinds + ["P"] * n_qm
    return w_perm, kinds


def kernel(x, mem, positions, mem_norm, w_mem_kv, norm_0, w_in_0, sinks_0, w_out_0, norm_1, w_in_1,
           w_out_1, norm_2, w_in_2, w_out_2, norm_3, w_in_3, sinks_3, w_out_3, final_norm):
    batch, seq, d = x.shape
    m = batch * seq
    layers = [(norm_0, w_in_0, w_out_0, sinks_0), (norm_1, w_in_1, w_out_1, None),
              (norm_2, w_in_2, w_out_2, None), (norm_3, w_in_3, w_out_3, sinks_3)]

    mkv = _mem_kv(mem.reshape(batch * N_MEM, d), mem_norm, w_mem_kv.astype(BF16))

    rope = (
        (_rope_tables(positions, SWA_HEAD_DIM, SWA_HEAD_DIM // ROPE_FRACTION, ROPE_THETA),
         SWA_HEAD_DIM // ROPE_FRACTION // 2),
        (_rope_tables(positions, MOBA_HEAD_DIM, MOBA_HEAD_DIM // ROPE_FRACTION, ROPE_THETA),
         MOBA_HEAD_DIM // ROPE_FRACTION // 2),
        (_rope_tables(positions, RET_QK_DIM, RET_QK_DIM, RET_THETA), RET_QK_DIM // 2),
    )

    tm_in = min(1024, m)
    tm_out = min(256, seq)
    h = x.reshape(m, d)
    n_layers = len(layers)
    for li, (g, w_in, w_out, sinks) in enumerate(layers):
        mixer = li % N_MIXERS
        w_perm, kinds = _permute_w_in(w_in, mixer)
        tabs, half = rope[mixer]
        ob, z = _in_proj(h, g, w_perm, kinds, tabs, half, RET_QK_DIM ** -0.5, tm_in)
        if mixer == 0:
            mix = _swa(ob, sinks, batch, seq, tq=min(256, seq))
        elif mixer == 1:
            mix = _moba(ob, batch, seq)
        else:
            mix = _ret(ob, batch, seq, seg=min(1024, seq))
        qm_blk = (ob.shape[1] - MEM_WIDTH) // MEM_WIDTH
        h = _out_proj(mix, ob, qm_blk, z, h, mkv, w_out.astype(BF16), final_norm, batch, seq,
                      tm_out, final=(li == n_layers - 1))
    return h.reshape(batch, seq, d)
```

```python
import functools
import math

import jax
import jax.numpy as jnp
from jax import lax
from jax.experimental import pallas as pl
from jax.experimental.pallas import tpu as pltpu

F32 = jnp.float32
BF16 = jnp.bfloat16

D_MODEL = 2048
N_MEM = 256
N_MIXERS = 3
BRANCH_WIDTH = D_MODEL
MEM_HEADS = 4
MEM_HEAD_DIM = 128
MEM_WIDTH = MEM_HEADS * MEM_HEAD_DIM
MIX_WIDTH = BRANCH_WIDTH - MEM_WIDTH

SWA_HEAD_DIM = 64
SWA_Q_HEADS = MIX_WIDTH // SWA_HEAD_DIM
SWA_KV_HEADS = SWA_Q_HEADS // 8
SWA_WINDOW = 128

MOBA_HEAD_DIM = 128
MOBA_HEADS = MIX_WIDTH // MOBA_HEAD_DIM
MOBA_BLOCK = 256
MOBA_TOPK = 3

RET_HEADS = 6
RET_V_DIM = MIX_WIDTH // RET_HEADS
RET_QK_DIM = RET_V_DIM // 2
RET_CHUNK = 128
RET_THETA = 10000.0

ROPE_THETA = 500000.0
ROPE_FRACTION = 4
EPS = 1e-6

LANES = 128
COL_TILE = 512
NEG = -1e30
VMEM_LIMIT = 56 * 1024 * 1024

_NT = (((1,), (1,)), ((), ()))
_TN = (((0,), (0,)), ((), ()))


def _cparams(n_axes):
    return pltpu.CompilerParams(dimension_semantics=("arbitrary",) * n_axes,
                                vmem_limit_bytes=VMEM_LIMIT)


def _silu(z):
    return z * (1.0 / (1.0 + jnp.exp(-z)))


def _rope_tables(positions, head_dim, rot_dim, theta):
    m = positions.size
    half = rot_dim // 2
    inv = theta ** (-jnp.arange(0, rot_dim, 2, dtype=F32) / rot_dim)
    ang = positions.astype(F32).reshape(m, 1) * inv
    cos, sin = jnp.cos(ang), jnp.sin(ang)
    rest = head_dim - rot_dim
    ones = jnp.ones((m, rest), F32)
    z_half = jnp.zeros((m, half), F32)
    z_rest = jnp.zeros((m, rest), F32)
    c = jnp.concatenate([cos, cos, ones], axis=1)
    s_plus = jnp.concatenate([z_half, sin, z_rest], axis=1)
    s_minus = jnp.concatenate([-sin, z_half, z_rest], axis=1)
    if 2 * half == LANES:
        s_plus, s_minus = s_plus + s_minus, jnp.zeros_like(s_minus)
    rep = LANES // head_dim
    tile = lambda t: jnp.tile(t, (1, rep)) if rep > 1 else t
    return tile(c), tile(s_plus), tile(s_minus)


def _in_proj_kernel(x_ref, g_ref, w_ref, c_ref, sp_ref, sm_ref, ob_ref, oz_ref, h_scr, *,
                    kinds, half, k_scale):
    x = x_ref[...]
    ms = jnp.mean(x * x, axis=-1, keepdims=True)
    h_scr[...] = ((x * lax.rsqrt(ms + EPS)) * g_ref[...]).astype(BF16)

    def rope(a):
        out = a * c_ref[...] + pltpu.roll(a, half, 1) * sp_ref[...]
        if 2 * half != LANES:
            out = out + pltpu.roll(a, LANES - half, 1) * sm_ref[...]
        return out

    gpt = COL_TILE // LANES
    n_b = ob_ref.shape[1] // COL_TILE
    n_z = oz_ref.shape[1] // COL_TILE
    for jt in range(n_b + n_z):
        acc = jnp.dot(h_scr[...], w_ref[:, jt * COL_TILE:(jt + 1) * COL_TILE],
                      preferred_element_type=F32)
        if jt >= n_b:
            oz_ref[:, (jt - n_b) * COL_TILE:(jt - n_b + 1) * COL_TILE] = acc
            continue
        for gi in range(gpt):
            kind = kinds[jt * gpt + gi]
            a = acc[:, gi * LANES:(gi + 1) * LANES]
            if kind in ("R", "RS"):
                a = rope(a)
            if kind == "RS":
                a = a * k_scale
            col = (jt * gpt + gi) * LANES
            ob_ref[:, col:col + LANES] = a.astype(BF16)


def _in_proj(x2d, g, w_perm, kinds, tabs, half, k_scale, tm):
    m, d = x2d.shape
    n_tot = w_perm.shape[1]
    n_bf = n_tot - BRANCH_WIDTH
    row = lambda i: (i, 0)
    fixed = lambda i: (0, 0)
    return pl.pallas_call(
        functools.partial(_in_proj_kernel, kinds=tuple(kinds), half=half, k_scale=k_scale),
        grid=(m // tm,),
        in_specs=[
            pl.BlockSpec((tm, d), row),
            pl.BlockSpec((1, d), fixed),
            pl.BlockSpec((d, n_tot), fixed, pipeline_mode=pl.Buffered(1)),
            pl.BlockSpec((tm, LANES), row),
            pl.BlockSpec((tm, LANES), row),
            pl.BlockSpec((tm, LANES), row),
        ],
        out_specs=[pl.BlockSpec((tm, n_bf), row), pl.BlockSpec((tm, BRANCH_WIDTH), row)],
        out_shape=[jax.ShapeDtypeStruct((m, n_bf), BF16),
                   jax.ShapeDtypeStruct((m, BRANCH_WIDTH), F32)],
        scratch_shapes=[pltpu.VMEM((tm, d), BF16)],
        compiler_params=_cparams(1),
        name="in_proj",
    )(x2d, g.reshape(1, d), w_perm, *tabs)


def _mem_kv_kernel(x_ref, g_ref, w_ref, o_ref):
    x = x_ref[...]
    ms = jnp.mean(x * x, axis=-1, keepdims=True)
    h = ((x * lax.rsqrt(ms + EPS)) * g_ref[...]).astype(BF16)
    o_ref[...] = jnp.dot(h, w_ref[...], preferred_element_type=F32).astype(BF16)


def _mem_kv(mem2d, g, w_bf16):
    m, d = mem2d.shape
    n = w_bf16.shape[1]
    tm = min(m, 256)
    return pl.pallas_call(
        _mem_kv_kernel,
        grid=(m // tm,),
        in_specs=[
            pl.BlockSpec((tm, d), lambda i: (i, 0)),
            pl.BlockSpec((1, d), lambda i: (0, 0)),
            pl.BlockSpec((d, n), lambda i: (0, 0)),
        ],
        out_specs=pl.BlockSpec((tm, n), lambda i: (i, 0)),
        out_shape=jax.ShapeDtypeStruct((m, n), BF16),
        compiler_params=_cparams(1),
        name="mem_kv",
    )(mem2d, g.reshape(1, d), w_bf16)


def _swa_kernel(sink_ref, q_ref, kv_ref, o_ref, *, tq):
    w = SWA_WINDOW
    dh = SWA_HEAD_DIM
    g_per = SWA_Q_HEADS // SWA_KV_HEADS
    t = pl.program_id(1)
    start = t * tq
    kstart = pl.multiple_of(jnp.maximum(start - w, 0), w)
    win = tq + w
    qpos = start + lax.broadcasted_iota(jnp.int32, (tq, win), 0)
    kpos = kstart + lax.broadcasted_iota(jnp.int32, (tq, win), 1)
    valid = (kpos <= qpos) & (kpos > qpos - w)
    scale = dh ** -0.5
    v_off = 2 * LANES
    for h in range(SWA_KV_HEADS):
        k_h = kv_ref[pl.ds(kstart, win), h * dh:(h + 1) * dh]
        v_h = kv_ref[pl.ds(kstart, win), v_off + h * dh:v_off + (h + 1) * dh]
        for pair in range(g_per // 2):
            outs = []
            for sub in range(2):
                head = h * g_per + pair * 2 + sub
                q_g = q_ref[:, head * dh:(head + 1) * dh]
                s = lax.dot_general(q_g, k_h, _NT, preferred_element_type=F32) * scale
                s = jnp.where(valid, s, NEG)
                sink = sink_ref[head]
                mx = jnp.maximum(jnp.max(s, axis=-1, keepdims=True), sink)
                p = jnp.exp(s - mx)
                denom = jnp.sum(p, axis=-1, keepdims=True) + jnp.exp(sink - mx)
                o = jnp.dot(p.astype(BF16), v_h, preferred_element_type=F32)
                outs.append(o / denom)
            c0 = (h * g_per + pair * 2) * dh
            o_ref[:, c0:c0 + 2 * dh] = jnp.concatenate(outs, axis=1).astype(BF16)


def _swa(ob, sinks, batch, seq, tq):
    m = batch * seq
    nq = SWA_Q_HEADS * SWA_HEAD_DIM
    n_t = seq // tq
    kv_blk = nq // COL_TILE
    return pl.pallas_call(
        functools.partial(_swa_kernel, tq=tq),
        grid=(batch, n_t),
        in_specs=[
            pl.BlockSpec(memory_space=pltpu.SMEM),
            pl.BlockSpec((tq, nq), lambda b, t: (b * n_t + t, 0)),
            pl.BlockSpec((seq, COL_TILE), lambda b, t: (b, kv_blk)),
        ],
        out_specs=pl.BlockSpec((tq, nq), lambda b, t: (b * n_t + t, 0)),
        out_shape=jax.ShapeDtypeStruct((m, nq), BF16),
        compiler_params=_cparams(2),
        name="swa",
    )(sinks.astype(F32), ob, ob)


MOBA_BIAS_ROWS = 16


def _moba_kernel(q_ref, k_ref, v_ref, o_ref, kaug, vt, kmean, qaug, *, seq, cb):
    blk = MOBA_BLOCK
    dh = MOBA_HEAD_DIM
    nblk = seq // blk
    kc = cb * blk
    nbr = MOBA_BIAS_ROWS
    i = pl.program_id(2)

    @pl.when(i == 0)
    def _():
        k = k_ref[...]
        kaug[:, :dh] = k
        row_blk = lax.broadcasted_iota(jnp.int32, (seq, LANES), 0) // blk
        lane = lax.broadcasted_iota(jnp.int32, (seq, LANES), 1)
        kaug[:, dh:] = (row_blk == lane).astype(BF16)
        kmean[...] = jnp.zeros_like(kmean)
        kmean[:nblk, :] = jnp.mean(k.astype(F32).reshape(nblk, blk, dh), axis=1)
        for c in range(seq // kc):
            vt[c] = v_ref[c * kc:(c + 1) * kc, :].astype(F32).T.astype(BF16)
        qaug[dh + nbr:, :] = jnp.zeros((dh - nbr, blk), BF16)

    q_t = q_ref[...].astype(F32).T.astype(BF16)
    gate = jnp.dot(kmean[...].astype(BF16), q_t, preferred_element_type=F32)
    row = lax.broadcasted_iota(jnp.int32, (nbr, blk), 0)
    rowf = row.astype(F32)
    past = row < i
    g = jnp.where(past, gate, -jnp.inf)
    sel = row == i
    for _ in range(min(MOBA_TOPK, nblk - 1)):
        mx = jnp.max(g, axis=0, keepdims=True)
        idx = jnp.min(jnp.where(g == mx, rowf, float(nbr)), axis=0, keepdims=True)
        hit = rowf == idx
        sel = sel | (hit & past)
        g = jnp.where(hit, -jnp.inf, g)
    qaug[:dh, :] = q_t
    qaug[dh:dh + nbr, :] = jnp.where(sel, 0.0, NEG).astype(BF16)
    scale = dh ** -0.5

    def scores(c):
        r0 = pl.multiple_of(c * kc, kc)
        return jnp.dot(kaug[pl.ds(r0, kc), :], qaug[...], preferred_element_type=F32) * scale

    c_own = i // cb
    s = scores(c_own)
    kpos = c_own * kc + lax.broadcasted_iota(jnp.int32, (kc, blk), 0)
    qpos = i * blk + lax.broadcasted_iota(jnp.int32, (kc, blk), 1)
    s = jnp.where(kpos <= qpos, s, NEG)
    m0 = jnp.max(s, axis=0, keepdims=True)
    p = jnp.exp(s - m0)
    l0 = jnp.sum(p, axis=0, keepdims=True)
    acc0 = jnp.dot(vt[c_own], p.astype(BF16), preferred_element_type=F32)

    def past_chunk(c, carry):
        m_prev, l_prev, acc = carry
        sc = scores(c)
        m_new = jnp.maximum(m_prev, jnp.max(sc, axis=0, keepdims=True))
        alpha = jnp.exp(m_prev - m_new)
        pc = jnp.exp(sc - m_new)
        l_new = alpha * l_prev + jnp.sum(pc, axis=0, keepdims=True)
        acc = alpha * acc + jnp.dot(vt[c], pc.astype(BF16), preferred_element_type=F32)
        return m_new, l_new, acc

    _, l_fin, acc = lax.fori_loop(0, c_own, past_chunk, (m0, l0, acc0))
    o_ref[...] = (acc / l_fin).T.astype(BF16)


def _moba(ob, batch, seq, cb):
    m = batch * seq
    blk = MOBA_BLOCK
    dh = MOBA_HEAD_DIM
    nh = MOBA_HEADS
    nblk = seq // blk
    kc = cb * blk
    return pl.pallas_call(
        functools.partial(_moba_kernel, seq=seq, cb=cb),
        grid=(batch, nh, nblk),
        in_specs=[
            pl.BlockSpec((blk, dh), lambda b, h, i: (b * nblk + i, h)),
            pl.BlockSpec((seq, dh), lambda b, h, i: (b, nh + h)),
            pl.BlockSpec((seq, dh), lambda b, h, i: (b, 2 * nh + h)),
        ],
        out_specs=pl.BlockSpec((blk, dh), lambda b, h, i: (b * nblk + i, h)),
        out_shape=jax.ShapeDtypeStruct((m, nh * dh), BF16),
        scratch_shapes=[pltpu.VMEM((seq, 2 * dh), BF16),
                        pltpu.VMEM((seq // kc, dh, kc), BF16),
                        pltpu.VMEM((MOBA_BIAS_ROWS, dh), F32),
                        pltpu.VMEM((2 * dh, blk), BF16)],
        compiler_params=_cparams(3),
        name="moba",
    )(ob, ob, ob)


def _ret_kernel(qk_ref, v_ref, dec_ref, xi_ref, zeta_ref, o_ref, r_scr, *, n_chunk, g_chunk):
    t = RET_CHUNK
    nh, dk, dv = RET_HEADS, RET_QK_DIM, RET_V_DIM

    @pl.when(pl.program_id(1) == 0)
    def _():
        r_scr[...] = jnp.zeros_like(r_scr)

    def body(c, carry):
        r0 = pl.multiple_of(c * t, t)
        for h in range(nh):
            q = qk_ref[pl.ds(r0, t), h * dk:(h + 1) * dk]
            k = qk_ref[pl.ds(r0, t), (nh + h) * dk:(nh + h + 1) * dk]
            v = v_ref[pl.ds(r0, t), h * dv:(h + 1) * dv]
            s = lax.dot_general(q, k, _NT, preferred_element_type=F32) * dec_ref[h]
            inner = jnp.dot(s.astype(BF16), v, preferred_element_type=F32)
            r_prev = r_scr[h]
            q_x = (q.astype(F32) * xi_ref[h]).astype(BF16)
            cross = jnp.dot(q_x, r_prev.astype(BF16), preferred_element_type=F32)
            o = inner + cross
            o = o * lax.rsqrt(jnp.mean(o * o, axis=-1, keepdims=True) + EPS)
            o_ref[pl.ds(r0, t), h * dv:(h + 1) * dv] = o.astype(BF16)
            k_z = (k.astype(F32) * zeta_ref[h]).astype(BF16)
            kv = lax.dot_general(k_z, v, _TN, preferred_element_type=F32)
            r_scr[h] = g_chunk[h] * r_prev + kv
        return carry

    lax.fori_loop(0, n_chunk, body, 0)


def _ret_constants():
    nh, t, dk = RET_HEADS, RET_CHUNK, RET_QK_DIM
    lin = [math.log(1.0 / 32) + (math.log(1.0 / 512) - math.log(1.0 / 32)) * h / (nh - 1)
           for h in range(nh)]
    log_g = [math.log1p(-math.exp(v)) for v in lin]
    i = jnp.arange(t, dtype=F32)
    lg = jnp.asarray(log_g, F32)
    diff = i[:, None] - i[None, :]
    decay = jnp.where(diff >= 0, jnp.exp(jnp.maximum(diff, 0.0)[None] * lg[:, None, None]), 0.0)
    xi = jnp.exp((i + 1)[None, :] * lg[:, None])
    zeta = jnp.exp((t - 1 - i)[None, :] * lg[:, None])
    xi_t = jnp.broadcast_to(xi[:, :, None], (nh, t, dk))
    zeta_t = jnp.broadcast_to(zeta[:, :, None], (nh, t, dk))
    g_chunk = tuple(math.exp(t * v) for v in log_g)
    return decay, xi_t, zeta_t, g_chunk


def _ret(ob, batch, seq, seg):
    m = batch * seq
    nh, dk, dv, t = RET_HEADS, RET_QK_DIM, RET_V_DIM, RET_CHUNK
    n_seg = seq // seg
    decay, xi_t, zeta_t, g_chunk = _ret_constants()
    wqk = 2 * nh * dk
    wv = nh * dv
    const = lambda b, s: (0, 0, 0)
    return pl.pallas_call(
        functools.partial(_ret_kernel, n_chunk=seg // t, g_chunk=g_chunk),
        grid=(batch, n_seg),
        in_specs=[
            pl.BlockSpec((seg, wqk), lambda b, s: (b * n_seg + s, 0)),
            pl.BlockSpec((seg, wv), lambda b, s: (b * n_seg + s, wqk // wv)),
            pl.BlockSpec((nh, t, t), const),
            pl.BlockSpec((nh, t, dk), const),
            pl.BlockSpec((nh, t, dk), const),
        ],
        out_specs=pl.BlockSpec((seg, wv), lambda b, s: (b * n_seg + s, 0)),
        out_shape=jax.ShapeDtypeStruct((m, wv), BF16),
        scratch_shapes=[pltpu.VMEM((nh, dk, dv), F32)],
        compiler_params=_cparams(2),
        name="retention",
    )(ob, ob, decay, xi_t, zeta_t)


def _out_kernel(mix_ref, qm_ref, z_ref, x_ref, mk_ref, mv_ref, w_ref, fn_ref, o_ref, y_scr, *,
                final):
    dm = MEM_HEAD_DIM
    scale = dm ** -0.5
    out = x_ref[...]
    for t in range(MIX_WIDTH // COL_TILE):
        lo, hi = t * COL_TILE, (t + 1) * COL_TILE
        y_t = (mix_ref[:, lo:hi].astype(F32) * _silu(z_ref[:, lo:hi])).astype(BF16)
        out = out + jnp.dot(y_t, w_ref[lo:hi, :], preferred_element_type=F32)
    for h in range(MEM_HEADS):
        lo, hi = h * dm, (h + 1) * dm
        s = lax.dot_general(qm_ref[:, lo:hi], mk_ref[:, lo:hi], _NT,
                            preferred_element_type=F32) * scale
        p = jnp.exp(s - jnp.max(s, axis=-1, keepdims=True))
        l = jnp.sum(p, axis=-1, keepdims=True)
        o = jnp.dot(p.astype(BF16), mv_ref[:, lo:hi], preferred_element_type=F32) / l
        y_scr[:, lo:hi] = (o * _silu(z_ref[:, MIX_WIDTH + lo:MIX_WIDTH + hi])).astype(BF16)
    out = out + jnp.dot(y_scr[...], w_ref[MIX_WIDTH:, :], preferred_element_type=F32)
    if final:
        ms = jnp.mean(out * out, axis=-1, keepdims=True)
        out = (out * lax.rsqrt(ms + EPS)) * fn_ref[...]
    o_ref[...] = out


def _out_proj(mix, ob, z, x2d, mkv, w_out, final_norm, batch, seq, tm, final):
    m, d = x2d.shape
    per_b = seq // tm
    row = lambda i: (i, 0)
    qm_blk = ob.shape[1] // MEM_WIDTH - 1
    return pl.pallas_call(
        functools.partial(_out_kernel, final=final),
        grid=(m // tm,),
        in_specs=[
            pl.BlockSpec((tm, MIX_WIDTH), row),
            pl.BlockSpec((tm, MEM_WIDTH), lambda i: (i, qm_blk)),
            pl.BlockSpec((tm, BRANCH_WIDTH), row),
            pl.BlockSpec((tm, d), row),
            pl.BlockSpec((N_MEM, MEM_WIDTH), lambda i: (i // per_b, 0)),
            pl.BlockSpec((N_MEM, MEM_WIDTH), lambda i: (i // per_b, 1)),
            pl.BlockSpec((BRANCH_WIDTH, d), lambda i: (0, 0)),
            pl.BlockSpec((1, d), lambda i: (0, 0)),
        ],
        out_specs=pl.BlockSpec((tm, d), row),
        out_shape=jax.ShapeDtypeStruct((m, d), F32),
        scratch_shapes=[pltpu.VMEM((tm, MEM_WIDTH), BF16)],
        compiler_params=_cparams(1),
        name="out_proj",
    )(mix, ob, z, x2d, mkv, mkv, w_out, final_norm.reshape(1, d))


def _permute_w_in(w_in, mixer):
    d = w_in.shape[0]
    n_mix = w_in.shape[1] - MEM_WIDTH - BRANCH_WIDTH
    w_qm = w_in[:, n_mix:n_mix + MEM_WIDTH]
    w_z = w_in[:, n_mix + MEM_WIDTH:]
    n_qm = MEM_WIDTH // LANES
    if mixer == 0:
        nq = SWA_Q_HEADS * SWA_HEAD_DIM
        nkv = SWA_KV_HEADS * SWA_HEAD_DIM
        pad = jnp.zeros((d, 2 * LANES - nkv), w_in.dtype)
        parts = [w_in[:, :nq], w_in[:, nq:nq + nkv], pad, w_in[:, nq + nkv:nq + 2 * nkv], pad]
        kinds = ["R"] * (nq // LANES) + ["R"] * 2 + ["P"] * 2
    elif mixer == 1:
        parts = [w_in[:, :n_mix]]
        kinds = ["R"] * (2 * MIX_WIDTH // LANES) + ["P"] * (MIX_WIDTH // LANES)
    else:
        nqk = RET_HEADS * RET_QK_DIM
        parts = [w_in[:, :n_mix]]
        kinds = ["R"] * (nqk // LANES) + ["RS"] * (nqk // LANES) + ["P"] * (MIX_WIDTH // LANES)
    w_perm = jnp.concatenate(parts + [w_qm, w_z], axis=1).astype(BF16)
    kinds = kinds + ["P"] * n_qm
    return w_perm, kinds


def kernel(x, mem, positions, mem_norm, w_mem_kv, norm_0, w_in_0, sinks_0, w_out_0, norm_1, w_in_1,
           w_out_1, norm_2, w_in_2, w_out_2, norm_3, w_in_3, sinks_3, w_out_3, final_norm):
    batch, seq, d = x.shape
    m = batch * seq
    layers = [(norm_0, w_in_0, w_out_0, sinks_0), (norm_1, w_in_1, w_out_1, None),
              (norm_2, w_in_2, w_out_2, None), (norm_3, w_in_3, w_out_3, sinks_3)]

    mkv = _mem_kv(mem.reshape(batch * N_MEM, d), mem_norm, w_mem_kv.astype(BF16))

    rope = (
        (_rope_tables(positions, SWA_HEAD_DIM, SWA_HEAD_DIM // ROPE_FRACTION, ROPE_THETA),
         SWA_HEAD_DIM // ROPE_FRACTION // 2),
        (_rope_tables(positions, MOBA_HEAD_DIM, MOBA_HEAD_DIM // ROPE_FRACTION, ROPE_THETA),
         MOBA_HEAD_DIM // ROPE_FRACTION // 2),
        (_rope_tables(positions, RET_QK_DIM, RET_QK_DIM, RET_THETA), RET_QK_DIM // 2),
    )

    tm_in = min(256, m)
    tm_out = min(256, seq)
    h = x.reshape(m, d)
    n_layers = len(layers)
    for li, (g, w_in, w_out, sinks) in enumerate(layers):
        mixer = li % N_MIXERS
        w_perm, kinds = _permute_w_in(w_in, mixer)
        tabs, half = rope[mixer]
        ob, z = _in_proj(h, g, w_perm, kinds, tabs, half, RET_QK_DIM ** -0.5, tm_in)
        if mixer == 0:
            mix = _swa(ob, sinks, batch, seq, tq=min(256, seq))
        elif mixer == 1:
            mix = _moba(ob, batch, seq, cb=min(4, seq // MOBA_BLOCK))
        else:
            mix = _ret(ob, batch, seq, seg=min(1024, seq))
        h = _out_proj(mix, ob, z, h, mkv, w_out.astype(BF16), final_norm, batch, seq, tm_out,
                      final=(li == n_layers - 1))
    return h.reshape(batch, seq, d)
```

```python
import functools
import math

import jax
import jax.numpy as jnp
from jax import lax
from jax.experimental import pallas as pl
from jax.experimental.pallas import tpu as pltpu

F32 = jnp.float32
BF16 = jnp.bfloat16

D_MODEL = 2048
N_MEM = 256
N_MIXERS = 3
BRANCH_WIDTH = D_MODEL
MEM_HEADS = 4
MEM_HEAD_DIM = 128
MEM_WIDTH = MEM_HEADS * MEM_HEAD_DIM
MIX_WIDTH = BRANCH_WIDTH - MEM_WIDTH

SWA_HEAD_DIM = 64
SWA_Q_HEADS = MIX_WIDTH // SWA_HEAD_DIM
SWA_KV_HEADS = SWA_Q_HEADS // 8
SWA_WINDOW = 128

MOBA_HEAD_DIM = 128
MOBA_HEADS = MIX_WIDTH // MOBA_HEAD_DIM
MOBA_BLOCK = 256
MOBA_TOPK = 3

RET_HEADS = 6
RET_V_DIM = MIX_WIDTH // RET_HEADS
RET_QK_DIM = RET_V_DIM // 2
RET_CHUNK = 128
RET_THETA = 10000.0

ROPE_THETA = 500000.0
ROPE_FRACTION = 4
EPS = 1e-6

LANES = 128
COL_TILE = 512
NEG = -1e30
VMEM_LIMIT = 56 * 1024 * 1024

_NT = (((1,), (1,)), ((), ()))
_TN = (((0,), (0,)), ((), ()))


def _cparams(n_axes):
    return pltpu.CompilerParams(dimension_semantics=("arbitrary",) * n_axes,
                                vmem_limit_bytes=VMEM_LIMIT)


def _silu(z):
    return z * (1.0 / (1.0 + jnp.exp(-z)))


def _rope_tables(positions, head_dim, rot_dim, theta):
    m = positions.size
    half = rot_dim // 2
    inv = theta ** (-jnp.arange(0, rot_dim, 2, dtype=F32) / rot_dim)
    ang = positions.astype(F32).reshape(m, 1) * inv
    cos, sin = jnp.cos(ang), jnp.sin(ang)
    rest = head_dim - rot_dim
    ones = jnp.ones((m, rest), F32)
    z_half = jnp.zeros((m, half), F32)
    z_rest = jnp.zeros((m, rest), F32)
    c = jnp.concatenate([cos, cos, ones], axis=1)
    s_plus = jnp.concatenate([z_half, sin, z_rest], axis=1)
    s_minus = jnp.concatenate([-sin, z_half, z_rest], axis=1)
    if 2 * half == LANES:
        s_plus, s_minus = s_plus + s_minus, jnp.zeros_like(s_minus)
    rep = LANES // head_dim
    tile = lambda t: jnp.tile(t, (1, rep)) if rep > 1 else t
    return tile(c), tile(s_plus), tile(s_minus)


def _in_proj_kernel(x_ref, g_ref, w_ref, c_ref, sp_ref, sm_ref, ob_ref, oz_ref, h_scr, *,
                    kinds, half, k_scale):
    x = x_ref[...]
    ms = jnp.mean(x * x, axis=-1, keepdims=True)
    h_scr[...] = ((x * lax.rsqrt(ms + EPS)) * g_ref[...]).astype(BF16)

    def rope(a):
        out = a * c_ref[...] + pltpu.roll(a, half, 1) * sp_ref[...]
        if 2 * half != LANES:
            out = out + pltpu.roll(a, LANES - half, 1) * sm_ref[...]
        return out

    gpt = COL_TILE // LANES
    n_b = ob_ref.shape[1] // COL_TILE
    n_z = oz_ref.shape[1] // COL_TILE
    for jt in range(n_b + n_z):
        acc = jnp.dot(h_scr[...], w_ref[:, jt * COL_TILE:(jt + 1) * COL_TILE],
                      preferred_element_type=F32)
        if jt >= n_b:
            oz_ref[:, (jt - n_b) * COL_TILE:(jt - n_b + 1) * COL_TILE] = acc
            continue
        for gi in range(gpt):
            kind = kinds[jt * gpt + gi]
            a = acc[:, gi * LANES:(gi + 1) * LANES]
            if kind in ("R", "RS"):
                a = rope(a)
            if kind == "RS":
                a = a * k_scale
            col = (jt * gpt + gi) * LANES
            ob_ref[:, col:col + LANES] = a.astype(BF16)


def _in_proj(x2d, g, w_perm, kinds, tabs, half, k_scale, tm):
    m, d = x2d.shape
    n_tot = w_perm.shape[1]
    n_bf = n_tot - BRANCH_WIDTH
    row = lambda i: (i, 0)
    fixed = lambda i: (0, 0)
    return pl.pallas_call(
        functools.partial(_in_proj_kernel, kinds=tuple(kinds), half=half, k_scale=k_scale),
        grid=(m // tm,),
        in_specs=[
            pl.BlockSpec((tm, d), row),
            pl.BlockSpec((1, d), fixed),
            pl.BlockSpec((d, n_tot), fixed, pipeline_mode=pl.Buffered(1)),
            pl.BlockSpec((tm, LANES), row),
            pl.BlockSpec((tm, LANES), row),
            pl.BlockSpec((tm, LANES), row),
        ],
        out_specs=[pl.BlockSpec((tm, n_bf), row), pl.BlockSpec((tm, BRANCH_WIDTH), row)],
        out_shape=[jax.ShapeDtypeStruct((m, n_bf), BF16),
                   jax.ShapeDtypeStruct((m, BRANCH_WIDTH), F32)],
        scratch_shapes=[pltpu.VMEM((tm, d), BF16)],
        compiler_params=_cparams(1),
        name="in_proj",
    )(x2d, g.reshape(1, d), w_perm, *tabs)


def _mem_kv_kernel(x_ref, g_ref, w_ref, o_ref):
    x = x_ref[...]
    ms = jnp.mean(x * x, axis=-1, keepdims=True)
    h = ((x * lax.rsqrt(ms + EPS)) * g_ref[...]).astype(BF16)
    o_ref[...] = jnp.dot(h, w_ref[...], preferred_element_type=F32).astype(BF16)


def _mem_kv(mem2d, g, w_bf16):
    m, d = mem2d.shape
    n = w_bf16.shape[1]
    tm = min(m, 256)
    return pl.pallas_call(
        _mem_kv_kernel,
        grid=(m // tm,),
        in_specs=[
            pl.BlockSpec((tm, d), lambda i: (i, 0)),
            pl.BlockSpec((1, d), lambda i: (0, 0)),
            pl.BlockSpec((d, n), lambda i: (0, 0)),
        ],
        out_specs=pl.BlockSpec((tm, n), lambda i: (i, 0)),
        out_shape=jax.ShapeDtypeStruct((m, n), BF16),
        compiler_params=_cparams(1),
        name="mem_kv",
    )(mem2d, g.reshape(1, d), w_bf16)


SWA_Q_SCALE = SWA_HEAD_DIM ** -0.5 * math.log2(math.e)


def _swa_kernel(sink_ref, q_ref, kv_ref, o_ref, vt, *, seq):
    w = SWA_WINDOW
    dh = SWA_HEAD_DIM
    g_per = SWA_Q_HEADS // SWA_KV_HEADS
    t = pl.program_id(1)

    @pl.when(t == 0)
    def _():
        for j in range(seq // w):
            vt[j] = kv_ref[j * w:(j + 1) * w, 2 * LANES:].astype(F32).T.astype(BF16)

    j0 = jnp.maximum(t - 1, 0)
    r0 = pl.multiple_of(j0 * w, w)
    k_win = kv_ref[pl.ds(r0, 2 * w), :2 * LANES]
    vt_win = jnp.concatenate([vt[j0], vt[j0 + 1]], axis=1)
    q_t = q_ref[...].astype(F32).T.astype(BF16)
    kpos = r0 + lax.broadcasted_iota(jnp.int32, (2 * w, w), 0)
    qpos = t * w + lax.broadcasted_iota(jnp.int32, (2 * w, w), 1)
    bias = jnp.where((kpos <= qpos) & (kpos > qpos - w), 0.0, NEG)
    bias = jnp.concatenate([bias] * g_per, axis=1)
    zeros = jnp.zeros((dh, g_per * w), BF16)
    for h in range(SWA_KV_HEADS):
        q_grp = jnp.concatenate(
            [q_t[(h * g_per + g) * dh:(h * g_per + g + 1) * dh, :] for g in range(g_per)], axis=1)
        q_pad = jnp.concatenate([q_grp, zeros] if h % 2 == 0 else [zeros, q_grp], axis=0)
        k_pair = k_win[:, (h // 2) * LANES:(h // 2 + 1) * LANES]
        s = jnp.dot(k_pair, q_pad, preferred_element_type=F32) + bias
        sink = sink_ref[h:h + 1, :]
        mx = jnp.maximum(jnp.max(s, axis=0, keepdims=True), sink)
        p = jnp.exp2(s - mx)
        denom = jnp.sum(p, axis=0, keepdims=True) + jnp.exp2(sink - mx)
        o_t = jnp.dot(vt_win[h * dh:(h + 1) * dh, :], p.astype(BF16),
                      preferred_element_type=F32) / denom
        for pair in range(g_per // 2):
            two = jnp.concatenate([o_t[:, (2 * pair) * w:(2 * pair + 1) * w],
                                   o_t[:, (2 * pair + 1) * w:(2 * pair + 2) * w]], axis=0)
            c0 = (h * g_per + 2 * pair) * dh
            o_ref[:, c0:c0 + 2 * dh] = two.T.astype(BF16)


def _swa(ob, sinks, batch, seq):
    m = batch * seq
    w = SWA_WINDOW
    nq = SWA_Q_HEADS * SWA_HEAD_DIM
    g_per = SWA_Q_HEADS // SWA_KV_HEADS
    n_t = seq // w
    kv_blk = nq // COL_TILE
    sink_rows = jnp.repeat(sinks.astype(F32) * math.log2(math.e), w).reshape(SWA_KV_HEADS, g_per * w)
    return pl.pallas_call(
        functools.partial(_swa_kernel, seq=seq),
        grid=(batch, n_t),
        in_specs=[
            pl.BlockSpec((SWA_KV_HEADS, g_per * w), lambda b, t: (0, 0)),
            pl.BlockSpec((w, nq), lambda b, t: (b * n_t + t, 0)),
            pl.BlockSpec((seq, COL_TILE), lambda b, t: (b, kv_blk)),
        ],
        out_specs=pl.BlockSpec((w, nq), lambda b, t: (b * n_t + t, 0)),
        out_shape=jax.ShapeDtypeStruct((m, nq), BF16),
        scratch_shapes=[pltpu.VMEM((seq // w, 2 * LANES, w), BF16)],
        compiler_params=_cparams(2),
        name="swa",
    )(sink_rows, ob, ob)


MOBA_BIAS_ROWS = 16
MOBA_Q_SCALE = MOBA_HEAD_DIM ** -0.5 * math.log2(math.e)


def _moba_kernel(q_ref, k_ref, v_ref, o_ref, kaug, vt, kmean, qaug, s_a, s_b, *, seq, cb, qb):
    blk = MOBA_BLOCK
    dh = MOBA_HEAD_DIM
    nblk = seq // blk
    kc = cb * blk
    wq = qb * blk
    nbr = MOBA_BIAS_ROWS
    i = pl.program_id(2)

    @pl.when(i == 0)
    def _():
        k = k_ref[...]
        kaug[:, :dh] = k
        row_blk = lax.broadcasted_iota(jnp.int32, (seq, LANES), 0) // blk
        lane = lax.broadcasted_iota(jnp.int32, (seq, LANES), 1)
        kaug[:, dh:] = (row_blk == lane).astype(BF16)
        kmean[...] = jnp.zeros_like(kmean)
        kmean[:nblk, :] = jnp.mean(k.astype(F32).reshape(nblk, blk, dh), axis=1)
        for c in range(seq // kc):
            vt[:, c * kc:(c + 1) * kc] = v_ref[c * kc:(c + 1) * kc, :].astype(F32).T.astype(BF16)
        qaug[dh + nbr:, :] = jnp.zeros((dh - nbr, wq), BF16)

    q_t = q_ref[...].astype(F32).T.astype(BF16)
    gate = jnp.dot(kmean[...].astype(BF16), q_t, preferred_element_type=F32)
    row = lax.broadcasted_iota(jnp.int32, (nbr, wq), 0)
    rowf = row.astype(F32)
    own = i * qb + lax.broadcasted_iota(jnp.int32, (nbr, wq), 1) // blk
    past = row < own
    g = jnp.where(past, gate, -jnp.inf)
    sel = row == own
    for _ in range(min(MOBA_TOPK, nblk - 1)):
        mx = jnp.max(g, axis=0, keepdims=True)
        idx = jnp.min(jnp.where(g == mx, rowf, float(nbr)), axis=0, keepdims=True)
        hit = rowf == idx
        sel = sel | (hit & past)
        g = jnp.where(hit, -jnp.inf, g)
    qaug[:dh, :] = q_t
    qaug[dh:dh + nbr, :] = jnp.where(sel, 0.0, NEG).astype(BF16)

    c_own = (i * qb) // cb
    qpos = i * wq + lax.broadcasted_iota(jnp.int32, (kc, wq), 1)
    for case in range(nblk // cb):
        @pl.when(c_own == case)
        def _(case=case):
            order = [case] + list(range(case))
            s_bufs = (s_a, s_b)

            def stage_scores(idx):
                c = order[idx]
                s = jnp.dot(kaug[c * kc:(c + 1) * kc, :], qaug[...],
                            preferred_element_type=F32)
                if c == case:
                    kpos = c * kc + lax.broadcasted_iota(jnp.int32, (kc, wq), 0)
                    s = jnp.where(kpos <= qpos, s, NEG)
                s_bufs[idx % 2][...] = s

            stage_scores(0)
            m_col = l_col = acc = None
            for idx, c in enumerate(order):
                if idx + 1 < len(order):
                    stage_scores(idx + 1)
                rows = slice(c * kc, (c + 1) * kc)
                s = s_bufs[idx % 2][...]
                m_c = jnp.max(s, axis=0, keepdims=True)
                if m_col is None:
                    m_col = m_c
                    p = jnp.exp2(s - m_col)
                    l_col = jnp.sum(p, axis=0, keepdims=True)
                    acc = jnp.dot(vt[:, rows], p.astype(BF16), preferred_element_type=F32)
                else:
                    m_new = jnp.maximum(m_col, m_c)
                    alpha = jnp.exp2(m_col - m_new)
                    p = jnp.exp2(s - m_new)
                    l_col = alpha * l_col + jnp.sum(p, axis=0, keepdims=True)
                    acc = alpha * acc + jnp.dot(vt[:, rows], p.astype(BF16),
                                                preferred_element_type=F32)
                    m_col = m_new
            o_ref[...] = (acc / l_col).T.astype(BF16)


def _moba(ob, batch, seq, cb, qb):
    m = batch * seq
    blk = MOBA_BLOCK
    dh = MOBA_HEAD_DIM
    nh = MOBA_HEADS
    n_q = seq // (qb * blk)
    wq = qb * blk
    return pl.pallas_call(
        functools.partial(_moba_kernel, seq=seq, cb=cb, qb=qb),
        grid=(batch, nh, n_q),
        in_specs=[
            pl.BlockSpec((wq, dh), lambda b, h, i: (b * n_q + i, h)),
            pl.BlockSpec((seq, dh), lambda b, h, i: (b, nh + h)),
            pl.BlockSpec((seq, dh), lambda b, h, i: (b, 2 * nh + h)),
        ],
        out_specs=pl.BlockSpec((wq, dh), lambda b, h, i: (b * n_q + i, h)),
        out_shape=jax.ShapeDtypeStruct((m, nh * dh), BF16),
        scratch_shapes=[pltpu.VMEM((seq, 2 * dh), BF16),
                        pltpu.VMEM((dh, seq), BF16),
                        pltpu.VMEM((MOBA_BIAS_ROWS, dh), F32),
                        pltpu.VMEM((2 * dh, wq), BF16),
                        pltpu.VMEM((cb * blk, wq), F32),
                        pltpu.VMEM((cb * blk, wq), F32)],
        compiler_params=_cparams(3),
        name="moba",
    )(ob, ob, ob)


def _ret_kernel(qk_ref, v_ref, dec_ref, xi_ref, zeta_ref, o_ref, r_scr, *, n_chunk, g_chunk):
    t = RET_CHUNK
    nh, dk, dv = RET_HEADS, RET_QK_DIM, RET_V_DIM

    @pl.when(pl.program_id(1) == 0)
    def _():
        r_scr[...] = jnp.zeros_like(r_scr)

    def body(c, carry):
        r0 = pl.multiple_of(c * t, t)
        for h in range(nh):
            q = qk_ref[pl.ds(r0, t), h * dk:(h + 1) * dk]
            k = qk_ref[pl.ds(r0, t), (nh + h) * dk:(nh + h + 1) * dk]
            v = v_ref[pl.ds(r0, t), h * dv:(h + 1) * dv]
            s = lax.dot_general(q, k, _NT, preferred_element_type=F32) * dec_ref[h]
            inner = jnp.dot(s.astype(BF16), v, preferred_element_type=F32)
            r_prev = r_scr[h]
            q_x = (q.astype(F32) * xi_ref[h]).astype(BF16)
            cross = jnp.dot(q_x, r_prev.astype(BF16), preferred_element_type=F32)
            o = inner + cross
            o = o * lax.rsqrt(jnp.mean(o * o, axis=-1, keepdims=True) + EPS)
            o_ref[pl.ds(r0, t), h * dv:(h + 1) * dv] = o.astype(BF16)
            k_z = (k.astype(F32) * zeta_ref[h]).astype(BF16)
            kv = lax.dot_general(k_z, v, _TN, preferred_element_type=F32)
            r_scr[h] = g_chunk[h] * r_prev + kv
        return carry

    lax.fori_loop(0, n_chunk, body, 0)


def _ret_constants():
    nh, t, dk = RET_HEADS, RET_CHUNK, RET_QK_DIM
    lin = [math.log(1.0 / 32) + (math.log(1.0 / 512) - math.log(1.0 / 32)) * h / (nh - 1)
           for h in range(nh)]
    log_g = [math.log1p(-math.exp(v)) for v in lin]
    i = jnp.arange(t, dtype=F32)
    lg = jnp.asarray(log_g, F32)
    diff = i[:, None] - i[None, :]
    decay = jnp.where(diff >= 0, jnp.exp(jnp.maximum(diff, 0.0)[None] * lg[:, None, None]), 0.0)
    xi = jnp.exp((i + 1)[None, :] * lg[:, None])
    zeta = jnp.exp((t - 1 - i)[None, :] * lg[:, None])
    xi_t = jnp.broadcast_to(xi[:, :, None], (nh, t, dk))
    zeta_t = jnp.broadcast_to(zeta[:, :, None], (nh, t, dk))
    g_chunk = tuple(math.exp(t * v) for v in log_g)
    return decay, xi_t, zeta_t, g_chunk


def _ret(ob, batch, seq, seg):
    m = batch * seq
    nh, dk, dv, t = RET_HEADS, RET_QK_DIM, RET_V_DIM, RET_CHUNK
    n_seg = seq // seg
    decay, xi_t, zeta_t, g_chunk = _ret_constants()
    wqk = 2 * nh * dk
    wv = nh * dv
    const = lambda b, s: (0, 0, 0)
    return pl.pallas_call(
        functools.partial(_ret_kernel, n_chunk=seg // t, g_chunk=g_chunk),
        grid=(batch, n_seg),
        in_specs=[
            pl.BlockSpec((seg, wqk), lambda b, s: (b * n_seg + s, 0)),
            pl.BlockSpec((seg, wv), lambda b, s: (b * n_seg + s, wqk // wv)),
            pl.BlockSpec((nh, t, t), const),
            pl.BlockSpec((nh, t, dk), const),
            pl.BlockSpec((nh, t, dk), const),
        ],
        out_specs=pl.BlockSpec((seg, wv), lambda b, s: (b * n_seg + s, 0)),
        out_shape=jax.ShapeDtypeStruct((m, wv), BF16),
        scratch_shapes=[pltpu.VMEM((nh, dk, dv), F32)],
        compiler_params=_cparams(2),
        name="retention",
    )(ob, ob, decay, xi_t, zeta_t)


def _out_kernel(mix_ref, qm_ref, z_ref, x_ref, mk_ref, mv_ref, w_ref, fn_ref, o_ref, y_scr, *,
                final):
    dm = MEM_HEAD_DIM
    scale = dm ** -0.5
    out = x_ref[...]
    for t in range(MIX_WIDTH // COL_TILE):
        lo, hi = t * COL_TILE, (t + 1) * COL_TILE
        y_t = (mix_ref[:, lo:hi].astype(F32) * _silu(z_ref[:, lo:hi])).astype(BF16)
        out = out + jnp.dot(y_t, w_ref[lo:hi, :], preferred_element_type=F32)
    for h in range(MEM_HEADS):
        lo, hi = h * dm, (h + 1) * dm
        s = lax.dot_general(qm_ref[:, lo:hi], mk_ref[:, lo:hi], _NT,
                            preferred_element_type=F32) * scale
        p = jnp.exp(s - jnp.max(s, axis=-1, keepdims=True))
        l = jnp.sum(p, axis=-1, keepdims=True)
        o = jnp.dot(p.astype(BF16), mv_ref[:, lo:hi], preferred_element_type=F32) / l
        y_scr[:, lo:hi] = (o * _silu(z_ref[:, MIX_WIDTH + lo:MIX_WIDTH + hi])).astype(BF16)
    out = out + jnp.dot(y_scr[...], w_ref[MIX_WIDTH:, :], preferred_element_type=F32)
    if final:
        ms = jnp.mean(out * out, axis=-1, keepdims=True)
        out = (out * lax.rsqrt(ms + EPS)) * fn_ref[...]
    o_ref[...] = out


def _out_proj(mix, ob, z, x2d, mkv, w_out, final_norm, batch, seq, tm, final):
    m, d = x2d.shape
    per_b = seq // tm
    row = lambda i: (i, 0)
    qm_blk = ob.shape[1] // MEM_WIDTH - 1
    return pl.pallas_call(
        functools.partial(_out_kernel, final=final),
        grid=(m // tm,),
        in_specs=[
            pl.BlockSpec((tm, MIX_WIDTH), row),
            pl.BlockSpec((tm, MEM_WIDTH), lambda i: (i, qm_blk)),
            pl.BlockSpec((tm, BRANCH_WIDTH), row),
            pl.BlockSpec((tm, d), row),
            pl.BlockSpec((N_MEM, MEM_WIDTH), lambda i: (i // per_b, 0)),
            pl.BlockSpec((N_MEM, MEM_WIDTH), lambda i: (i // per_b, 1)),
            pl.BlockSpec((BRANCH_WIDTH, d), lambda i: (0, 0)),
            pl.BlockSpec((1, d), lambda i: (0, 0)),
        ],
        out_specs=pl.BlockSpec((tm, d), row),
        out_shape=jax.ShapeDtypeStruct((m, d), F32),
        scratch_shapes=[pltpu.VMEM((tm, MEM_WIDTH), BF16)],
        compiler_params=_cparams(1),
        name="out_proj",
    )(mix, ob, z, x2d, mkv, mkv, w_out, final_norm.reshape(1, d))


def _permute_w_in(w_in, mixer):
    d = w_in.shape[0]
    n_mix = w_in.shape[1] - MEM_WIDTH - BRANCH_WIDTH
    w_qm = w_in[:, n_mix:n_mix + MEM_WIDTH]
    w_z = w_in[:, n_mix + MEM_WIDTH:]
    n_qm = MEM_WIDTH // LANES
    if mixer == 0:
        nq = SWA_Q_HEADS * SWA_HEAD_DIM
        nkv = SWA_KV_HEADS * SWA_HEAD_DIM
        pad = jnp.zeros((d, 2 * LANES - nkv), w_in.dtype)
        parts = [w_in[:, :nq], w_in[:, nq:nq + nkv], pad, w_in[:, nq + nkv:nq + 2 * nkv], pad]
        kinds = ["RS"] * (nq // LANES) + ["R"] * 2 + ["P"] * 2
    elif mixer == 1:
        parts = [w_in[:, :n_mix]]
        kinds = (["RS"] * (MIX_WIDTH // LANES) + ["R"] * (MIX_WIDTH // LANES)
                 + ["P"] * (MIX_WIDTH // LANES))
    else:
        nqk = RET_HEADS * RET_QK_DIM
        parts = [w_in[:, :n_mix]]
        kinds = ["R"] * (nqk // LANES) + ["RS"] * (nqk // LANES) + ["P"] * (MIX_WIDTH // LANES)
    w_perm = jnp.concatenate(parts + [w_qm, w_z], axis=1).astype(BF16)
    kinds = kinds + ["P"] * n_qm
    return w_perm, kinds


def kernel(x, mem, positions, mem_norm, w_mem_kv, norm_0, w_in_0, sinks_0, w_out_0, norm_1, w_in_1,
           w_out_1, norm_2, w_in_2, w_out_2, norm_3, w_in_3, sinks_3, w_out_3, final_norm):
    batch, seq, d = x.shape
    m = batch * seq
    layers = [(norm_0, w_in_0, w_out_0, sinks_0), (norm_1, w_in_1, w_out_1, None),
              (norm_2, w_in_2, w_out_2, None), (norm_3, w_in_3, w_out_3, sinks_3)]

    mkv = _mem_kv(mem.reshape(batch * N_MEM, d), mem_norm, w_mem_kv.astype(BF16))

    rope = (
        (_rope_tables(positions, SWA_HEAD_DIM, SWA_HEAD_DIM // ROPE_FRACTION, ROPE_THETA),
         SWA_HEAD_DIM // ROPE_FRACTION // 2),
        (_rope_tables(positions, MOBA_HEAD_DIM, MOBA_HEAD_DIM // ROPE_FRACTION, ROPE_THETA),
         MOBA_HEAD_DIM // ROPE_FRACTION // 2),
        (_rope_tables(positions, RET_QK_DIM, RET_QK_DIM, RET_THETA), RET_QK_DIM // 2),
    )

    tm_in = min(256, m)
    tm_out = min(256, seq)
    h = x.reshape(m, d)
    n_layers = len(layers)
    for li, (g, w_in, w_out, sinks) in enumerate(layers):
        mixer = li % N_MIXERS
        w_perm, kinds = _permute_w_in(w_in, mixer)
        tabs, half = rope[mixer]
        rs_scale = (SWA_Q_SCALE, MOBA_Q_SCALE, RET_QK_DIM ** -0.5)[mixer]
        ob, z = _in_proj(h, g, w_perm, kinds, tabs, half, rs_scale, tm_in)
        if mixer == 0:
            mix = _swa(ob, sinks, batch, seq)
        elif mixer == 1:
            mix = _moba(ob, batch, seq, cb=min(4, seq // MOBA_BLOCK), qb=2)
        else:
            mix = _ret(ob, batch, seq, seg=min(1024, seq))
        h = _out_proj(mix, ob, z, h, mkv, w_out.astype(BF16), final_norm, batch, seq, tm_out,
                      final=(li == n_layers - 1))
    return h.reshape(batch, seq, d)
```

```python
import functools
import math

import jax
import jax.numpy as jnp
from jax import lax
from jax.experimental import pallas as pl
from jax.experimental.pallas import tpu as pltpu

F32 = jnp.float32
BF16 = jnp.bfloat16

D_MODEL = 2048
N_MEM = 256
N_MIXERS = 3
BRANCH_WIDTH = D_MODEL
MEM_HEADS = 4
MEM_HEAD_DIM = 128
MEM_WIDTH = MEM_HEADS * MEM_HEAD_DIM
MIX_WIDTH = BRANCH_WIDTH - MEM_WIDTH

SWA_HEAD_DIM = 64
SWA_Q_HEADS = MIX_WIDTH // SWA_HEAD_DIM
SWA_KV_HEADS = SWA_Q_HEADS // 8
SWA_WINDOW = 128

MOBA_HEAD_DIM = 128
MOBA_HEADS = MIX_WIDTH // MOBA_HEAD_DIM
MOBA_BLOCK = 256
MOBA_TOPK = 3

RET_HEADS = 6
RET_V_DIM = MIX_WIDTH // RET_HEADS
RET_QK_DIM = RET_V_DIM // 2
RET_CHUNK = 128
RET_THETA = 10000.0

ROPE_THETA = 500000.0
ROPE_FRACTION = 4
EPS = 1e-6

LANES = 128
COL_TILE = 512
NEG = -1e30
VMEM_LIMIT = 56 * 1024 * 1024

_NT = (((1,), (1,)), ((), ()))
_TN = (((0,), (0,)), ((), ()))


def _cparams(n_axes):
    return pltpu.CompilerParams(dimension_semantics=("arbitrary",) * n_axes,
                                vmem_limit_bytes=VMEM_LIMIT)


def _silu(z):
    return z * (1.0 / (1.0 + jnp.exp(-z)))


def _in_proj_kernel(x_ref, g_ref, w_ref, pos_ref, lane_ref, ob_ref, oz_ref, h_scr, *,
                    plan, half, rs_scale):
    x = x_ref[...]
    ms = jnp.mean(x * x, axis=-1, keepdims=True)
    h_scr[...] = ((x * lax.rsqrt(ms + EPS)) * g_ref[...]).astype(BF16)

    tables = {}

    def rope(a, pat):
        if pat not in tables:
            ang = pos_ref[...] * lane_ref[3 * pat:3 * pat + 1, :]
            sn = jnp.sin(ang)
            tables[pat] = (jnp.cos(ang), sn * lane_ref[3 * pat + 1:3 * pat + 2, :],
                           sn * lane_ref[3 * pat + 2:3 * pat + 3, :])
        c, s_plus, s_minus = tables[pat]
        out = a * c + pltpu.roll(a, half, 1) * s_plus
        if 2 * half != LANES:
            out = out + pltpu.roll(a, LANES - half, 1) * s_minus
        return out

    gpt = COL_TILE // LANES
    n_grp = len(plan)
    for g0 in range(0, n_grp, gpt):
        g1 = min(g0 + gpt, n_grp)
        acc = jnp.dot(h_scr[...], w_ref[:, g0 * LANES:g1 * LANES], preferred_element_type=F32)
        for gi in range(g0, g1):
            kind, dest = plan[gi]
            a = acc[:, (gi - g0) * LANES:(gi - g0 + 1) * LANES]
            if dest is None:
                zc = (gi - (n_grp - oz_ref.shape[1] // LANES)) * LANES
                oz_ref[:, zc:zc + LANES] = a
                continue
            if kind in ("R", "RS", "Rh"):
                a = rope(a, 1 if kind == "Rh" else 0)
            if kind == "RS":
                a = a * rs_scale
            ob_ref[:, dest * LANES:(dest + 1) * LANES] = a.astype(BF16)


def _rope_lane_rows(head_dim, rot_dim, theta):
    half = rot_dim // 2
    inv = theta ** (-jnp.arange(0, rot_dim, 2, dtype=F32) / rot_dim)
    rest = head_dim - rot_dim
    z_half, z_rest = jnp.zeros((half,), F32), jnp.zeros((rest,), F32)
    ones = jnp.ones((half,), F32)
    inv_h = jnp.concatenate([inv, inv, z_rest])
    plus_h = jnp.concatenate([z_half, ones, z_rest])
    minus_h = jnp.concatenate([-ones, z_half, z_rest])
    if 2 * half == LANES:
        plus_h, minus_h = plus_h + minus_h, jnp.zeros_like(minus_h)
    rep = LANES // head_dim
    full = [jnp.tile(r, rep) for r in (inv_h, plus_h, minus_h)]
    first = [jnp.concatenate([r, jnp.zeros((LANES - head_dim,), F32)])
             for r in (inv_h, plus_h, minus_h)]
    return jnp.stack(full + first)


def _in_proj(x2d, g, w_bf16, pos_col, lane_rows, plan, half, rs_scale, tm):
    m, d = x2d.shape
    n_tot = w_bf16.shape[1]
    n_bf = (max(dest for _, dest in plan if dest is not None) + 1) * LANES
    row = lambda i: (i, 0)
    fixed = lambda i: (0, 0)
    return pl.pallas_call(
        functools.partial(_in_proj_kernel, plan=tuple(plan), half=half, rs_scale=rs_scale),
        grid=(m // tm,),
        in_specs=[
            pl.BlockSpec((tm, d), row),
            pl.BlockSpec((1, d), fixed),
            pl.BlockSpec((d, n_tot), fixed, pipeline_mode=pl.Buffered(1)),
            pl.BlockSpec((tm, 1), row),
            pl.BlockSpec(lane_rows.shape, fixed),
        ],
        out_specs=[pl.BlockSpec((tm, n_bf), row), pl.BlockSpec((tm, BRANCH_WIDTH), row)],
        out_shape=[jax.ShapeDtypeStruct((m, n_bf), BF16),
                   jax.ShapeDtypeStruct((m, BRANCH_WIDTH), F32)],
        scratch_shapes=[pltpu.VMEM((tm, d), BF16)],
        compiler_params=_cparams(1),
        name="in_proj",
    )(x2d, g.reshape(1, d), w_bf16, pos_col, lane_rows)


def _mem_kv_kernel(x_ref, g_ref, w_ref, o_ref):
    x = x_ref[...]
    ms = jnp.mean(x * x, axis=-1, keepdims=True)
    h = ((x * lax.rsqrt(ms + EPS)) * g_ref[...]).astype(BF16)
    o_ref[...] = jnp.dot(h, w_ref[...], preferred_element_type=F32).astype(BF16)


def _mem_kv(mem2d, g, w_bf16):
    m, d = mem2d.shape
    n = w_bf16.shape[1]
    tm = min(m, 256)
    return pl.pallas_call(
        _mem_kv_kernel,
        grid=(m // tm,),
        in_specs=[
            pl.BlockSpec((tm, d), lambda i: (i, 0)),
            pl.BlockSpec((1, d), lambda i: (0, 0)),
            pl.BlockSpec((d, n), lambda i: (0, 0)),
        ],
        out_specs=pl.BlockSpec((tm, n), lambda i: (i, 0)),
        out_shape=jax.ShapeDtypeStruct((m, n), BF16),
        compiler_params=_cparams(1),
        name="mem_kv",
    )(mem2d, g.reshape(1, d), w_bf16)


SWA_Q_SCALE = SWA_HEAD_DIM ** -0.5 * math.log2(math.e)


def _swa_kernel(sink_ref, q_ref, ka_ref, kb_ref, kc_ref, o_ref, vt, *, seq):
    w = SWA_WINDOW
    dh = SWA_HEAD_DIM
    g_per = SWA_Q_HEADS // SWA_KV_HEADS
    t = pl.program_id(1)

    @pl.when(t == 0)
    def _():
        for j in range(seq // w):
            rows = slice(j * w, (j + 1) * w)
            both = jnp.concatenate([kb_ref[rows, :], kc_ref[rows, :]], axis=1)
            vt[j] = both.astype(F32).T.astype(BF16)

    j0 = jnp.maximum(t - 1, 0)
    r0 = pl.multiple_of(j0 * w, w)
    k_wins = (ka_ref[pl.ds(r0, 2 * w), :], kb_ref[pl.ds(r0, 2 * w), :])
    vt_win = jnp.concatenate([vt[j0], vt[j0 + 1]], axis=1)
    q_t = q_ref[...].astype(F32).T.astype(BF16)
    kpos = r0 + lax.broadcasted_iota(jnp.int32, (2 * w, w), 0)
    qpos = t * w + lax.broadcasted_iota(jnp.int32, (2 * w, w), 1)
    bias = jnp.where((kpos <= qpos) & (kpos > qpos - w), 0.0, NEG)
    bias = jnp.concatenate([bias] * g_per, axis=1)
    zeros = jnp.zeros((dh, g_per * w), BF16)
    for h in range(SWA_KV_HEADS):
        q_grp = jnp.concatenate(
            [q_t[(h * g_per + g) * dh:(h * g_per + g + 1) * dh, :] for g in range(g_per)], axis=1)
        q_pad = jnp.concatenate([q_grp, zeros] if h % 2 == 0 else [zeros, q_grp], axis=0)
        s = jnp.dot(k_wins[h // 2], q_pad, preferred_element_type=F32) + bias
        sink = sink_ref[h:h + 1, :]
        mx = jnp.maximum(jnp.max(s, axis=0, keepdims=True), sink)
        p = jnp.exp2(s - mx)
        denom = jnp.sum(p, axis=0, keepdims=True) + jnp.exp2(sink - mx)
        o_t = jnp.dot(vt_win[(h + 1) * dh:(h + 2) * dh, :], p.astype(BF16),
                      preferred_element_type=F32) / denom
        for pair in range(g_per // 2):
            two = jnp.concatenate([o_t[:, (2 * pair) * w:(2 * pair + 1) * w],
                                   o_t[:, (2 * pair + 1) * w:(2 * pair + 2) * w]], axis=0)
            c0 = (h * g_per + 2 * pair) * dh
            o_ref[:, c0:c0 + 2 * dh] = two.T.astype(BF16)


def _swa(ob, sinks, batch, seq):
    m = batch * seq
    w = SWA_WINDOW
    nq = SWA_Q_HEADS * SWA_HEAD_DIM
    g_per = SWA_Q_HEADS // SWA_KV_HEADS
    n_t = seq // w
    kv_grp = (nq + MEM_WIDTH) // LANES
    sink_rows = jnp.repeat(sinks.astype(F32) * math.log2(math.e), w).reshape(SWA_KV_HEADS, g_per * w)
    kv_spec = lambda j: pl.BlockSpec((seq, LANES), lambda b, t: (b, kv_grp + j))
    return pl.pallas_call(
        functools.partial(_swa_kernel, seq=seq),
        grid=(batch, n_t),
        in_specs=[
            pl.BlockSpec((SWA_KV_HEADS, g_per * w), lambda b, t: (0, 0)),
            pl.BlockSpec((w, nq), lambda b, t: (b * n_t + t, 0)),
            kv_spec(0), kv_spec(1), kv_spec(2),
        ],
        out_specs=pl.BlockSpec((w, nq), lambda b, t: (b * n_t + t, 0)),
        out_shape=jax.ShapeDtypeStruct((m, nq), BF16),
        scratch_shapes=[pltpu.VMEM((seq // w, 2 * LANES, w), BF16)],
        compiler_params=_cparams(2),
        name="swa",
    )(sink_rows, ob, ob, ob, ob)


MOBA_BIAS_ROWS = 16
MOBA_Q_SCALE = MOBA_HEAD_DIM ** -0.5 * math.log2(math.e)


def _moba_kernel(q_ref, k_ref, v_ref, o_ref, kaug, vt, kmean, qaug, s_a, s_b, *, seq, cb, qb):
    blk = MOBA_BLOCK
    dh = MOBA_HEAD_DIM
    nblk = seq // blk
    kc = cb * blk
    wq = qb * blk
    nbr = MOBA_BIAS_ROWS
    i = pl.program_id(2)

    @pl.when(i == 0)
    def _():
        k = k_ref[...]
        kaug[:, :dh] = k
        row_blk = lax.broadcasted_iota(jnp.int32, (seq, LANES), 0) // blk
        lane = lax.broadcasted_iota(jnp.int32, (seq, LANES), 1)
        kaug[:, dh:] = (row_blk == lane).astype(BF16)
        kmean[...] = jnp.zeros_like(kmean)
        kmean[:nblk, :] = jnp.mean(k.astype(F32).reshape(nblk, blk, dh), axis=1)
        for c in range(seq // kc):
            vt[:, c * kc:(c + 1) * kc] = v_ref[c * kc:(c + 1) * kc, :].astype(F32).T.astype(BF16)
        qaug[dh + nbr:, :] = jnp.zeros((dh - nbr, wq), BF16)

    q_t = q_ref[...].astype(F32).T.astype(BF16)
    gate = jnp.dot(kmean[...].astype(BF16), q_t, preferred_element_type=F32)
    row = lax.broadcasted_iota(jnp.int32, (nbr, wq), 0)
    rowf = row.astype(F32)
    own = i * qb + lax.broadcasted_iota(jnp.int32, (nbr, wq), 1) // blk
    past = row < own
    g = jnp.where(past, gate, -jnp.inf)
    sel = row == own
    for _ in range(min(MOBA_TOPK, nblk - 1)):
        mx = jnp.max(g, axis=0, keepdims=True)
        idx = jnp.min(jnp.where(g == mx, rowf, float(nbr)), axis=0, keepdims=True)
        hit = rowf == idx
        sel = sel | (hit & past)
        g = jnp.where(hit, -jnp.inf, g)
    qaug[:dh, :] = q_t
    qaug[dh:dh + nbr, :] = jnp.where(sel, 0.0, NEG).astype(BF16)

    c_own = (i * qb) // cb
    qpos = i * wq + lax.broadcasted_iota(jnp.int32, (kc, wq), 1)
    for case in range(nblk // cb):
        @pl.when(c_own == case)
        def _(case=case):
            order = [case] + list(range(case))
            s_bufs = (s_a, s_b)

            def stage_scores(idx):
                c = order[idx]
                s = jnp.dot(kaug[c * kc:(c + 1) * kc, :], qaug[...],
                            preferred_element_type=F32)
                if c == case:
                    kpos = c * kc + lax.broadcasted_iota(jnp.int32, (kc, wq), 0)
                    s = jnp.where(kpos <= qpos, s, NEG)
                s_bufs[idx % 2][...] = s

            stage_scores(0)
            m_col = l_col = acc = None
            for idx, c in enumerate(order):
                if idx + 1 < len(order):
                    stage_scores(idx + 1)
                rows = slice(c * kc, (c + 1) * kc)
                s = s_bufs[idx % 2][...]
                m_c = jnp.max(s, axis=0, keepdims=True)
                if m_col is None:
                    m_col = m_c
                    p = jnp.exp2(s - m_col)
                    l_col = jnp.sum(p, axis=0, keepdims=True)
                    acc = jnp.dot(vt[:, rows], p.astype(BF16), preferred_element_type=F32)
                else:
                    m_new = jnp.maximum(m_col, m_c)
                    alpha = jnp.exp2(m_col - m_new)
                    p = jnp.exp2(s - m_new)
                    l_col = alpha * l_col + jnp.sum(p, axis=0, keepdims=True)
                    acc = alpha * acc + jnp.dot(vt[:, rows], p.astype(BF16),
                                                preferred_element_type=F32)
                    m_col = m_new
            o_ref[...] = (acc / l_col).T.astype(BF16)


def _moba(ob, batch, seq, cb, qb):
    m = batch * seq
    blk = MOBA_BLOCK
    dh = MOBA_HEAD_DIM
    nh = MOBA_HEADS
    n_q = seq // (qb * blk)
    wq = qb * blk
    return pl.pallas_call(
        functools.partial(_moba_kernel, seq=seq, cb=cb, qb=qb),
        grid=(batch, nh, n_q),
        in_specs=[
            pl.BlockSpec((wq, dh), lambda b, h, i: (b * n_q + i, h)),
            pl.BlockSpec((seq, dh), lambda b, h, i: (b, nh + h)),
            pl.BlockSpec((seq, dh), lambda b, h, i: (b, 2 * nh + h)),
        ],
        out_specs=pl.BlockSpec((wq, dh), lambda b, h, i: (b * n_q + i, h)),
        out_shape=jax.ShapeDtypeStruct((m, nh * dh), BF16),
        scratch_shapes=[pltpu.VMEM((seq, 2 * dh), BF16),
                        pltpu.VMEM((dh, seq), BF16),
                        pltpu.VMEM((MOBA_BIAS_ROWS, dh), F32),
                        pltpu.VMEM((2 * dh, wq), BF16),
                        pltpu.VMEM((cb * blk, wq), F32),
                        pltpu.VMEM((cb * blk, wq), F32)],
        compiler_params=_cparams(3),
        name="moba",
    )(ob, ob, ob)


def _ret_kernel(qk_ref, v_ref, dec_ref, xi_ref, zeta_ref, o_ref, r_scr, *, n_chunk, g_chunk):
    t = RET_CHUNK
    nh, dk, dv = RET_HEADS, RET_QK_DIM, RET_V_DIM

    @pl.when(pl.program_id(1) == 0)
    def _():
        r_scr[...] = jnp.zeros_like(r_scr)

    def body(c, carry):
        r0 = pl.multiple_of(c * t, t)
        for h in range(nh):
            q = qk_ref[pl.ds(r0, t), h * dk:(h + 1) * dk]
            k = qk_ref[pl.ds(r0, t), (nh + h) * dk:(nh + h + 1) * dk]
            v = v_ref[pl.ds(r0, t), h * dv:(h + 1) * dv]
            s = lax.dot_general(q, k, _NT, preferred_element_type=F32) * dec_ref[h]
            inner = jnp.dot(s.astype(BF16), v, preferred_element_type=F32)
            r_prev = r_scr[h]
            q_x = (q.astype(F32) * xi_ref[h]).astype(BF16)
            cross = jnp.dot(q_x, r_prev.astype(BF16), preferred_element_type=F32)
            o = inner + cross
            o = o * lax.rsqrt(jnp.mean(o * o, axis=-1, keepdims=True) + EPS)
            o_ref[pl.ds(r0, t), h * dv:(h + 1) * dv] = o.astype(BF16)
            k_z = (k.astype(F32) * zeta_ref[h]).astype(BF16)
            kv = lax.dot_general(k_z, v, _TN, preferred_element_type=F32)
            r_scr[h] = g_chunk[h] * r_prev + kv
        return carry

    lax.fori_loop(0, n_chunk, body, 0)


def _ret_constants():
    nh, t, dk = RET_HEADS, RET_CHUNK, RET_QK_DIM
    lin = [math.log(1.0 / 32) + (math.log(1.0 / 512) - math.log(1.0 / 32)) * h / (nh - 1)
           for h in range(nh)]
    log_g = [math.log1p(-math.exp(v)) for v in lin]
    i = jnp.arange(t, dtype=F32)
    lg = jnp.asarray(log_g, F32)
    diff = i[:, None] - i[None, :]
    decay = jnp.where(diff >= 0, jnp.exp(jnp.maximum(diff, 0.0)[None] * lg[:, None, None]), 0.0)
    xi = jnp.exp((i + 1)[None, :] * lg[:, None])
    zeta = jnp.exp((t - 1 - i)[None, :] * lg[:, None])
    xi_t = jnp.broadcast_to(xi[:, :, None], (nh, t, dk))
    zeta_t = jnp.broadcast_to(zeta[:, :, None], (nh, t, dk))
    g_chunk = tuple(math.exp(t * v) for v in log_g)
    return decay, xi_t, zeta_t, g_chunk


def _ret(ob, batch, seq, seg):
    m = batch * seq
    nh, dk, dv, t = RET_HEADS, RET_QK_DIM, RET_V_DIM, RET_CHUNK
    n_seg = seq // seg
    decay, xi_t, zeta_t, g_chunk = _ret_constants()
    wqk = 2 * nh * dk
    wv = nh * dv
    const = lambda b, s: (0, 0, 0)
    return pl.pallas_call(
        functools.partial(_ret_kernel, n_chunk=seg // t, g_chunk=g_chunk),
        grid=(batch, n_seg),
        in_specs=[
            pl.BlockSpec((seg, wqk), lambda b, s: (b * n_seg + s, 0)),
            pl.BlockSpec((seg, wv), lambda b, s: (b * n_seg + s, wqk // wv)),
            pl.BlockSpec((nh, t, t), const),
            pl.BlockSpec((nh, t, dk), const),
            pl.BlockSpec((nh, t, dk), const),
        ],
        out_specs=pl.BlockSpec((seg, wv), lambda b, s: (b * n_seg + s, 0)),
        out_shape=jax.ShapeDtypeStruct((m, wv), BF16),
        scratch_shapes=[pltpu.VMEM((nh, dk, dv), F32)],
        compiler_params=_cparams(2),
        name="retention",
    )(ob, ob, decay, xi_t, zeta_t)


def _out_kernel(mix_ref, qm_ref, z_ref, x_ref, mk_ref, mv_ref, w_ref, fn_ref, o_ref, y_scr, *,
                final):
    dm = MEM_HEAD_DIM
    scale = dm ** -0.5
    out = x_ref[...]
    for t in range(MIX_WIDTH // COL_TILE):
        lo, hi = t * COL_TILE, (t + 1) * COL_TILE
        y_t = (mix_ref[:, lo:hi].astype(F32) * _silu(z_ref[:, lo:hi])).astype(BF16)
        out = out + jnp.dot(y_t, w_ref[lo:hi, :], preferred_element_type=F32)
    for h in range(MEM_HEADS):
        lo, hi = h * dm, (h + 1) * dm
        s = lax.dot_general(qm_ref[:, lo:hi], mk_ref[:, lo:hi], _NT,
                            preferred_element_type=F32) * scale
        p = jnp.exp(s - jnp.max(s, axis=-1, keepdims=True))
        l = jnp.sum(p, axis=-1, keepdims=True)
        o = jnp.dot(p.astype(BF16), mv_ref[:, lo:hi], preferred_element_type=F32) / l
        y_scr[:, lo:hi] = (o * _silu(z_ref[:, MIX_WIDTH + lo:MIX_WIDTH + hi])).astype(BF16)
    out = out + jnp.dot(y_scr[...], w_ref[MIX_WIDTH:, :], preferred_element_type=F32)
    if final:
        ms = jnp.mean(out * out, axis=-1, keepdims=True)
        out = (out * lax.rsqrt(ms + EPS)) * fn_ref[...]
    o_ref[...] = out


def _out_proj(mix, ob, qm_blk, z, x2d, mkv, w_out, final_norm, batch, seq, tm, final):
    m, d = x2d.shape
    per_b = seq // tm
    row = lambda i: (i, 0)
    return pl.pallas_call(
        functools.partial(_out_kernel, final=final),
        grid=(m // tm,),
        in_specs=[
            pl.BlockSpec((tm, MIX_WIDTH), row),
            pl.BlockSpec((tm, MEM_WIDTH), lambda i: (i, qm_blk)),
            pl.BlockSpec((tm, BRANCH_WIDTH), row),
            pl.BlockSpec((tm, d), row),
            pl.BlockSpec((N_MEM, MEM_WIDTH), lambda i: (i // per_b, 0)),
            pl.BlockSpec((N_MEM, MEM_WIDTH), lambda i: (i // per_b, 1)),
            pl.BlockSpec((BRANCH_WIDTH, d), lambda i: (0, 0), pipeline_mode=pl.Buffered(1)),
            pl.BlockSpec((1, d), lambda i: (0, 0)),
        ],
        out_specs=pl.BlockSpec((tm, d), row),
        out_shape=jax.ShapeDtypeStruct((m, d), F32),
        scratch_shapes=[pltpu.VMEM((tm, MEM_WIDTH), BF16)],
        compiler_params=_cparams(1),
        name="out_proj",
    )(mix, ob, z, x2d, mkv, mkv, w_out, final_norm.reshape(1, d))


def _in_proj_plan(mixer):
    n_qm = MEM_WIDTH // LANES
    n_z = BRANCH_WIDTH // LANES
    if mixer == 0:
        n_q = SWA_Q_HEADS * SWA_HEAD_DIM // LANES
        plan = [("RS", gq) for gq in range(n_q)]
        plan += [("R", n_q + n_qm), ("Rh", n_q + n_qm + 1), ("P", n_q + n_qm + 2)]
        plan += [("P", n_q + j) for j in range(n_qm)]
    elif mixer == 1:
        n_h = MIX_WIDTH // LANES
        kinds = ["RS"] * n_h + ["R"] * n_h + ["P"] * (n_h + n_qm)
        plan = [(kind, j) for j, kind in enumerate(kinds)]
    else:
        n_qk = RET_HEADS * RET_QK_DIM // LANES
        kinds = ["R"] * n_qk + ["RS"] * n_qk + ["P"] * (MIX_WIDTH // LANES + n_qm)
        plan = [(kind, j) for j, kind in enumerate(kinds)]
    return plan + [("P", None)] * n_z


def kernel(x, mem, positions, mem_norm, w_mem_kv, norm_0, w_in_0, sinks_0, w_out_0, norm_1, w_in_1,
           w_out_1, norm_2, w_in_2, w_out_2, norm_3, w_in_3, sinks_3, w_out_3, final_norm):
    batch, seq, d = x.shape
    m = batch * seq
    layers = [(norm_0, w_in_0, w_out_0, sinks_0), (norm_1, w_in_1, w_out_1, None),
              (norm_2, w_in_2, w_out_2, None), (norm_3, w_in_3, w_out_3, sinks_3)]

    mkv = _mem_kv(mem.reshape(batch * N_MEM, d), mem_norm, w_mem_kv.astype(BF16))

    pos_col = positions.astype(F32).reshape(m, 1)
    rope = (
        (_rope_lane_rows(SWA_HEAD_DIM, SWA_HEAD_DIM // ROPE_FRACTION, ROPE_THETA),
         SWA_HEAD_DIM // ROPE_FRACTION // 2),
        (_rope_lane_rows(MOBA_HEAD_DIM, MOBA_HEAD_DIM // ROPE_FRACTION, ROPE_THETA),
         MOBA_HEAD_DIM // ROPE_FRACTION // 2),
        (_rope_lane_rows(RET_QK_DIM, RET_QK_DIM, RET_THETA), RET_QK_DIM // 2),
    )

    tm_in = min(256, m)
    tm_out = min(512, seq)
    h = x.reshape(m, d)
    n_layers = len(layers)
    for li, (g, w_in, w_out, sinks) in enumerate(layers):
        mixer = li % N_MIXERS
        lane_rows, half = rope[mixer]
        rs_scale = (SWA_Q_SCALE, MOBA_Q_SCALE, RET_QK_DIM ** -0.5)[mixer]
        ob, z = _in_proj(h, g, w_in.astype(BF16), pos_col, lane_rows, _in_proj_plan(mixer), half,
                         rs_scale, tm_in)
        if mixer == 0:
            mix = _swa(ob, sinks, batch, seq)
            qm_blk = SWA_Q_HEADS * SWA_HEAD_DIM // MEM_WIDTH
        elif mixer == 1:
            mix = _moba(ob, batch, seq, cb=min(4, seq // MOBA_BLOCK), qb=2)
            qm_blk = 3 * MIX_WIDTH // MEM_WIDTH
        else:
            mix = _ret(ob, batch, seq, seg=min(1024, seq))
            qm_blk = (2 * RET_HEADS * RET_QK_DIM + MIX_WIDTH) // MEM_WIDTH
        h = _out_proj(mix, ob, qm_blk, z, h, mkv, w_out.astype(BF16), final_norm, batch, seq,
                      tm_out, final=(li == n_layers - 1))
    return h.reshape(batch, seq, d)
```

```python
import functools
import math

import jax
import jax.numpy as jnp
from jax import lax
from jax.experimental import pallas as pl
from jax.experimental.pallas import tpu as pltpu

F32 = jnp.float32
BF16 = jnp.bfloat16

D_MODEL = 2048
N_MEM = 256
N_MIXERS = 3
BRANCH_WIDTH = D_MODEL
MEM_HEADS = 4
MEM_HEAD_DIM = 128
MEM_WIDTH = MEM_HEADS * MEM_HEAD_DIM
MIX_WIDTH = BRANCH_WIDTH - MEM_WIDTH

SWA_HEAD_DIM = 64
SWA_Q_HEADS = MIX_WIDTH // SWA_HEAD_DIM
SWA_KV_HEADS = SWA_Q_HEADS // 8
SWA_WINDOW = 128

MOBA_HEAD_DIM = 128
MOBA_HEADS = MIX_WIDTH // MOBA_HEAD_DIM
MOBA_BLOCK = 256
MOBA_TOPK = 3

RET_HEADS = 6
RET_V_DIM = MIX_WIDTH // RET_HEADS
RET_QK_DIM = RET_V_DIM // 2
RET_CHUNK = 128
RET_THETA = 10000.0

ROPE_THETA = 500000.0
ROPE_FRACTION = 4
EPS = 1e-6

LANES = 128
COL_TILE = 512
NEG = -1e30
VMEM_LIMIT = 56 * 1024 * 1024

_NT = (((1,), (1,)), ((), ()))
_TN = (((0,), (0,)), ((), ()))


def _cparams(n_axes):
    return pltpu.CompilerParams(dimension_semantics=("arbitrary",) * n_axes,
                                vmem_limit_bytes=VMEM_LIMIT)


def _silu(z):
    return z * (1.0 / (1.0 + jnp.exp(-z)))


def _load_weight_bf16(w_hbm, w_scr, stage, sem, chunk):
    n_chunks = w_hbm.shape[1] // chunk

    def copy(c):
        cols = slice(c * chunk, (c + 1) * chunk)
        return pltpu.make_async_copy(w_hbm.at[:, cols], stage.at[c % 2], sem.at[c % 2])

    copy(0).start()
    for c in range(n_chunks):
        if c + 1 < n_chunks:
            copy(c + 1).start()
        copy(c).wait()
        w_scr[:, c * chunk:(c + 1) * chunk] = stage[c % 2].astype(BF16)


def _weight_chunk(n_cols):
    for groups in (2, 5):
        if n_cols % (groups * LANES) == 0:
            return groups * LANES
    return LANES


def _in_proj_kernel(x_ref, g_ref, w_hbm, pos_ref, lane_ref, ob_ref, oz_ref, h_scr, w_ref, stage,
                    sem, *, plan, half, rs_scale):
    @pl.when(pl.program_id(0) == 0)
    def _():
        _load_weight_bf16(w_hbm, w_ref, stage, sem, stage.shape[2])

    x = x_ref[...]
    ms = jnp.mean(x * x, axis=-1, keepdims=True)
    h_scr[...] = ((x * lax.rsqrt(ms + EPS)) * g_ref[...]).astype(BF16)

    tables = {}

    def rope(a, pat):
        if 0 not in tables:
            ang = pos_ref[...] * lane_ref[0:1, :]
            sn = jnp.sin(ang)
            tables[0] = (jnp.cos(ang), sn * lane_ref[1:2, :], sn * lane_ref[2:3, :])
        if pat not in tables:
            first = lax.broadcasted_iota(jnp.int32, tables[0][0].shape, 1) < LANES // 2
            c0, sp0, sm0 = tables[0]
            tables[pat] = (jnp.where(first, c0, 1.0), jnp.where(first, sp0, 0.0),
                           jnp.where(first, sm0, 0.0))
        c, s_plus, s_minus = tables[pat]
        out = a * c + pltpu.roll(a, half, 1) * s_plus
        if 2 * half != LANES:
            out = out + pltpu.roll(a, LANES - half, 1) * s_minus
        return out

    gpt = COL_TILE // LANES
    n_grp = len(plan)
    for g0 in range(0, n_grp, gpt):
        g1 = min(g0 + gpt, n_grp)
        acc = jnp.dot(h_scr[...], w_ref[:, g0 * LANES:g1 * LANES], preferred_element_type=F32)
        for gi in range(g0, g1):
            kind, dest = plan[gi]
            a = acc[:, (gi - g0) * LANES:(gi - g0 + 1) * LANES]
            if dest is None:
                zc = (gi - (n_grp - oz_ref.shape[1] // LANES)) * LANES
                oz_ref[:, zc:zc + LANES] = a
                continue
            if kind in ("R", "RS", "Rh"):
                a = rope(a, 1 if kind == "Rh" else 0)
            if kind == "RS":
                a = a * rs_scale
            ob_ref[:, dest * LANES:(dest + 1) * LANES] = a.astype(BF16)


def _rope_lane_rows(head_dim, rot_dim, theta):
    half = rot_dim // 2
    inv = theta ** (-jnp.arange(0, rot_dim, 2, dtype=F32) / rot_dim)
    rest = head_dim - rot_dim
    z_half, z_rest = jnp.zeros((half,), F32), jnp.zeros((rest,), F32)
    ones = jnp.ones((half,), F32)
    inv_h = jnp.concatenate([inv, inv, z_rest])
    plus_h = jnp.concatenate([z_half, ones, z_rest])
    minus_h = jnp.concatenate([-ones, z_half, z_rest])
    if 2 * half == LANES:
        plus_h, minus_h = plus_h + minus_h, jnp.zeros_like(minus_h)
    rep = LANES // head_dim
    return jnp.stack([jnp.tile(r, rep) for r in (inv_h, plus_h, minus_h)])


def _in_proj(x2d, g, w_in, pos_col, lane_rows, plan, half, rs_scale, tm):
    m, d = x2d.shape
    n_tot = w_in.shape[1]
    n_bf = (max(dest for _, dest in plan if dest is not None) + 1) * LANES
    row = lambda i: (i, 0)
    fixed = lambda i: (0, 0)
    return pl.pallas_call(
        functools.partial(_in_proj_kernel, plan=tuple(plan), half=half, rs_scale=rs_scale),
        grid=(m // tm,),
        in_specs=[
            pl.BlockSpec((tm, d), row),
            pl.BlockSpec((1, d), fixed),
            pl.BlockSpec(memory_space=pl.ANY),
            pl.BlockSpec((tm, 1), row),
            pl.BlockSpec(lane_rows.shape, fixed),
        ],
        out_specs=[pl.BlockSpec((tm, n_bf), row), pl.BlockSpec((tm, BRANCH_WIDTH), row)],
        out_shape=[jax.ShapeDtypeStruct((m, n_bf), BF16),
                   jax.ShapeDtypeStruct((m, BRANCH_WIDTH), F32)],
        scratch_shapes=[pltpu.VMEM((tm, d), BF16),
                        pltpu.VMEM((d, n_tot), BF16),
                        pltpu.VMEM((2, d, _weight_chunk(n_tot)), F32),
                        pltpu.SemaphoreType.DMA((2,))],
        compiler_params=_cparams(1),
        name="in_proj",
    )(x2d, g.reshape(1, d), w_in, pos_col, lane_rows)


def _mem_kv_kernel(x_ref, g_ref, w_ref, o_ref):
    x = x_ref[...]
    ms = jnp.mean(x * x, axis=-1, keepdims=True)
    h = ((x * lax.rsqrt(ms + EPS)) * g_ref[...]).astype(BF16)
    o_ref[...] = jnp.dot(h, w_ref[...], preferred_element_type=F32).astype(BF16)


def _mem_kv(mem2d, g, w_bf16):
    m, d = mem2d.shape
    n = w_bf16.shape[1]
    tm = min(m, 256)
    return pl.pallas_call(
        _mem_kv_kernel,
        grid=(m // tm,),
        in_specs=[
            pl.BlockSpec((tm, d), lambda i: (i, 0)),
            pl.BlockSpec((1, d), lambda i: (0, 0)),
            pl.BlockSpec((d, n), lambda i: (0, 0)),
        ],
        out_specs=pl.BlockSpec((tm, n), lambda i: (i, 0)),
        out_shape=jax.ShapeDtypeStruct((m, n), BF16),
        compiler_params=_cparams(1),
        name="mem_kv",
    )(mem2d, g.reshape(1, d), w_bf16)


SWA_Q_SCALE = SWA_HEAD_DIM ** -0.5 * math.log2(math.e)


def _swa_kernel(sink_ref, q_ref, ka_ref, kb_ref, kc_ref, o_ref, vt, s_scr, *, seq):
    w = SWA_WINDOW
    dh = SWA_HEAD_DIM
    g_per = SWA_Q_HEADS // SWA_KV_HEADS
    t = pl.program_id(1)

    @pl.when(t == 0)
    def _():
        for j in range(seq // w):
            rows = slice(j * w, (j + 1) * w)
            both = jnp.concatenate([kb_ref[rows, :], kc_ref[rows, :]], axis=1)
            vt[j] = both.astype(F32).T.astype(BF16)

    j0 = jnp.maximum(t - 1, 0)
    r0 = pl.multiple_of(j0 * w, w)
    k_wins = (ka_ref[pl.ds(r0, 2 * w), :], kb_ref[pl.ds(r0, 2 * w), :])
    vt_win = jnp.concatenate([vt[j0], vt[j0 + 1]], axis=1)
    q_t = q_ref[...].astype(F32).T.astype(BF16)
    kpos = r0 + lax.broadcasted_iota(jnp.int32, (2 * w, w), 0)
    qpos = t * w + lax.broadcasted_iota(jnp.int32, (2 * w, w), 1)
    bias = jnp.where((kpos <= qpos) & (kpos > qpos - w), 0.0, NEG)
    bias = jnp.concatenate([bias] * g_per, axis=1)
    zeros = jnp.zeros((dh, g_per * w), BF16)
    for h in range(SWA_KV_HEADS):
        q_grp = jnp.concatenate(
            [q_t[(h * g_per + g) * dh:(h * g_per + g + 1) * dh, :] for g in range(g_per)], axis=1)
        q_pad = jnp.concatenate([q_grp, zeros] if h % 2 == 0 else [zeros, q_grp], axis=0)
        s_scr[h] = jnp.dot(k_wins[h // 2], q_pad, preferred_element_type=F32) + bias
    for h in range(SWA_KV_HEADS):
        s = s_scr[h]
        sink = sink_ref[h:h + 1, :]
        mx = jnp.maximum(jnp.max(s, axis=0, keepdims=True), sink)
        p = jnp.exp2(s - mx)
        denom = jnp.sum(p, axis=0, keepdims=True) + jnp.exp2(sink - mx)
        o_t = jnp.dot(vt_win[(h + 1) * dh:(h + 2) * dh, :], p.astype(BF16),
                      preferred_element_type=F32) / denom
        for pair in range(g_per // 2):
            two = jnp.concatenate([o_t[:, (2 * pair) * w:(2 * pair + 1) * w],
                                   o_t[:, (2 * pair + 1) * w:(2 * pair + 2) * w]], axis=0)
            c0 = (h * g_per + 2 * pair) * dh
            o_ref[:, c0:c0 + 2 * dh] = two.T.astype(BF16)


def _swa(ob, sinks, batch, seq):
    m = batch * seq
    w = SWA_WINDOW
    nq = SWA_Q_HEADS * SWA_HEAD_DIM
    g_per = SWA_Q_HEADS // SWA_KV_HEADS
    n_t = seq // w
    kv_grp = (nq + MEM_WIDTH) // LANES
    sink_rows = jnp.repeat(sinks.astype(F32) * math.log2(math.e), w).reshape(SWA_KV_HEADS, g_per * w)
    kv_spec = lambda j: pl.BlockSpec((seq, LANES), lambda b, t: (b, kv_grp + j))
    return pl.pallas_call(
        functools.partial(_swa_kernel, seq=seq),
        grid=(batch, n_t),
        in_specs=[
            pl.BlockSpec((SWA_KV_HEADS, g_per * w), lambda b, t: (0, 0)),
            pl.BlockSpec((w, nq), lambda b, t: (b * n_t + t, 0)),
            kv_spec(0), kv_spec(1), kv_spec(2),
        ],
        out_specs=pl.BlockSpec((w, nq), lambda b, t: (b * n_t + t, 0)),
        out_shape=jax.ShapeDtypeStruct((m, nq), BF16),
        scratch_shapes=[pltpu.VMEM((seq // w, 2 * LANES, w), BF16),
                        pltpu.VMEM((SWA_KV_HEADS, 2 * w, g_per * w), F32)],
        compiler_params=_cparams(2),
        name="swa",
    )(sink_rows, ob, ob, ob, ob)


MOBA_BIAS_ROWS = 16
MOBA_Q_SCALE = MOBA_HEAD_DIM ** -0.5 * math.log2(math.e)


def _moba_kernel(q_ref, k_ref, v_ref, o_ref, kaug, vt, kmean, qaug, s_a, s_b, *, seq, cb, qb):
    blk = MOBA_BLOCK
    dh = MOBA_HEAD_DIM
    nblk = seq // blk
    kc = cb * blk
    wq = qb * blk
    nbr = MOBA_BIAS_ROWS
    i = pl.program_id(2)

    @pl.when(i == 0)
    def _():
        k = k_ref[...]
        kaug[:, :dh] = k
        row_blk = lax.broadcasted_iota(jnp.int32, (seq, LANES), 0) // blk
        lane = lax.broadcasted_iota(jnp.int32, (seq, LANES), 1)
        kaug[:, dh:] = (row_blk == lane).astype(BF16)
        kmean[...] = jnp.zeros_like(kmean)
        kmean[:nblk, :] = jnp.mean(k.astype(F32).reshape(nblk, blk, dh), axis=1)
        for c in range(seq // kc):
            vt[:, c * kc:(c + 1) * kc] = v_ref[c * kc:(c + 1) * kc, :].astype(F32).T.astype(BF16)
        qaug[dh + nbr:, :] = jnp.zeros((dh - nbr, wq), BF16)

    q_t = q_ref[...].astype(F32).T.astype(BF16)
    gate = jnp.dot(kmean[...].astype(BF16), q_t, preferred_element_type=F32)
    row = lax.broadcasted_iota(jnp.int32, (nbr, wq), 0)
    rowf = row.astype(F32)
    own = i * qb + lax.broadcasted_iota(jnp.int32, (nbr, wq), 1) // blk
    past = row < own
    g = jnp.where(past, gate, -jnp.inf)
    sel = row == own
    for _ in range(min(MOBA_TOPK, nblk - 1)):
        mx = jnp.max(g, axis=0, keepdims=True)
        idx = jnp.min(jnp.where(g == mx, rowf, float(nbr)), axis=0, keepdims=True)
        hit = rowf == idx
        sel = sel | (hit & past)
        g = jnp.where(hit, -jnp.inf, g)
    qaug[:dh, :] = q_t
    qaug[dh:dh + nbr, :] = jnp.where(sel, 0.0, NEG).astype(BF16)

    c_own = (i * qb) // cb
    qpos = i * wq + lax.broadcasted_iota(jnp.int32, (kc, wq), 1)
    for case in range(nblk // cb):
        @pl.when(c_own == case)
        def _(case=case):
            order = [case] + list(range(case))
            s_bufs = (s_a, s_b)

            def stage_scores(idx):
                c = order[idx]
                s = jnp.dot(kaug[c * kc:(c + 1) * kc, :], qaug[...],
                            preferred_element_type=F32)
                if c == case:
                    kpos = c * kc + lax.broadcasted_iota(jnp.int32, (kc, wq), 0)
                    s = jnp.where(kpos <= qpos, s, NEG)
                s_bufs[idx % 2][...] = s

            stage_scores(0)
            m_col = l_col = acc = None
            for idx, c in enumerate(order):
                if idx + 1 < len(order):
                    stage_scores(idx + 1)
                rows = slice(c * kc, (c + 1) * kc)
                s = s_bufs[idx % 2][...]
                m_c = jnp.max(s, axis=0, keepdims=True)
                if m_col is None:
                    m_col = m_c
                    p = jnp.exp2(s - m_col)
                    l_col = jnp.sum(p, axis=0, keepdims=True)
                    acc = jnp.dot(vt[:, rows], p.astype(BF16), preferred_element_type=F32)
                else:
                    m_new = jnp.maximum(m_col, m_c)
                    alpha = jnp.exp2(m_col - m_new)
                    p = jnp.exp2(s - m_new)
                    l_col = alpha * l_col + jnp.sum(p, axis=0, keepdims=True)
                    acc = alpha * acc + jnp.dot(vt[:, rows], p.astype(BF16),
                                                preferred_element_type=F32)
                    m_col = m_new
            o_ref[...] = (acc / l_col).T.astype(BF16)


def _moba(ob, batch, seq, cb, qb):
    m = batch * seq
    blk = MOBA_BLOCK
    dh = MOBA_HEAD_DIM
    nh = MOBA_HEADS
    n_q = seq // (qb * blk)
    wq = qb * blk
    return pl.pallas_call(
        functools.partial(_moba_kernel, seq=seq, cb=cb, qb=qb),
        grid=(batch, nh, n_q),
        in_specs=[
            pl.BlockSpec((wq, dh), lambda b, h, i: (b * n_q + i, h)),
            pl.BlockSpec((seq, dh), lambda b, h, i: (b, nh + h)),
            pl.BlockSpec((seq, dh), lambda b, h, i: (b, 2 * nh + h)),
        ],
        out_specs=pl.BlockSpec((wq, dh), lambda b, h, i: (b * n_q + i, h)),
        out_shape=jax.ShapeDtypeStruct((m, nh * dh), BF16),
        scratch_shapes=[pltpu.VMEM((seq, 2 * dh), BF16),
                        pltpu.VMEM((dh, seq), BF16),
                        pltpu.VMEM((MOBA_BIAS_ROWS, dh), F32),
                        pltpu.VMEM((2 * dh, wq), BF16),
                        pltpu.VMEM((cb * blk, wq), F32),
                        pltpu.VMEM((cb * blk, wq), F32)],
        compiler_params=_cparams(3),
        name="moba",
    )(ob, ob, ob)


def _ret_kernel(qk_ref, v_ref, dec_ref, xi_ref, zeta_ref, o_ref, r_scr, *, n_chunk, g_chunk):
    t = RET_CHUNK
    nh, dk, dv = RET_HEADS, RET_QK_DIM, RET_V_DIM

    @pl.when(pl.program_id(1) == 0)
    def _():
        r_scr[...] = jnp.zeros_like(r_scr)

    def body(c, carry):
        r0 = pl.multiple_of(c * t, t)
        for h in range(nh):
            q = qk_ref[pl.ds(r0, t), h * dk:(h + 1) * dk]
            k = qk_ref[pl.ds(r0, t), (nh + h) * dk:(nh + h + 1) * dk]
            v = v_ref[pl.ds(r0, t), h * dv:(h + 1) * dv]
            s = lax.dot_general(q, k, _NT, preferred_element_type=F32) * dec_ref[h]
            inner = jnp.dot(s.astype(BF16), v, preferred_element_type=F32)
            r_prev = r_scr[h]
            q_x = (q.astype(F32) * xi_ref[h]).astype(BF16)
            cross = jnp.dot(q_x, r_prev.astype(BF16), preferred_element_type=F32)
            o = inner + cross
            o = o * lax.rsqrt(jnp.mean(o * o, axis=-1, keepdims=True) + EPS)
            o_ref[pl.ds(r0, t), h * dv:(h + 1) * dv] = o.astype(BF16)
            k_z = (k.astype(F32) * zeta_ref[h]).astype(BF16)
            kv = lax.dot_general(k_z, v, _TN, preferred_element_type=F32)
            r_scr[h] = g_chunk[h] * r_prev + kv
        return carry

    lax.fori_loop(0, n_chunk, body, 0)


def _ret_constants():
    nh, t, dk = RET_HEADS, RET_CHUNK, RET_QK_DIM
    lin = [math.log(1.0 / 32) + (math.log(1.0 / 512) - math.log(1.0 / 32)) * h / (nh - 1)
           for h in range(nh)]
    log_g = [math.log1p(-math.exp(v)) for v in lin]
    i = jnp.arange(t, dtype=F32)
    lg = jnp.asarray(log_g, F32)
    diff = i[:, None] - i[None, :]
    decay = jnp.where(diff >= 0, jnp.exp(jnp.maximum(diff, 0.0)[None] * lg[:, None, None]), 0.0)
    xi = jnp.exp((i + 1)[None, :] * lg[:, None])
    zeta = jnp.exp((t - 1 - i)[None, :] * lg[:, None])
    xi_t = jnp.broadcast_to(xi[:, :, None], (nh, t, dk))
    zeta_t = jnp.broadcast_to(zeta[:, :, None], (nh, t, dk))
    g_chunk = tuple(math.exp(t * v) for v in log_g)
    return decay, xi_t, zeta_t, g_chunk


def _ret(ob, batch, seq, seg):
    m = batch * seq
    nh, dk, dv, t = RET_HEADS, RET_QK_DIM, RET_V_DIM, RET_CHUNK
    n_seg = seq // seg
    decay, xi_t, zeta_t, g_chunk = _ret_constants()
    wqk = 2 * nh * dk
    wv = nh * dv
    const = lambda b, s: (0, 0, 0)
    return pl.pallas_call(
        functools.partial(_ret_kernel, n_chunk=seg // t, g_chunk=g_chunk),
        grid=(batch, n_seg),
        in_specs=[
            pl.BlockSpec((seg, wqk), lambda b, s: (b * n_seg + s, 0)),
            pl.BlockSpec((seg, wv), lambda b, s: (b * n_seg + s, wqk // wv)),
            pl.BlockSpec((nh, t, t), const),
            pl.BlockSpec((nh, t, dk), const),
            pl.BlockSpec((nh, t, dk), const),
        ],
        out_specs=pl.BlockSpec((seg, wv), lambda b, s: (b * n_seg + s, 0)),
        out_shape=jax.ShapeDtypeStruct((m, wv), BF16),
        scratch_shapes=[pltpu.VMEM((nh, dk, dv), F32)],
        compiler_params=_cparams(2),
        name="retention",
    )(ob, ob, decay, xi_t, zeta_t)


def _out_kernel(mix_ref, qm_ref, z_ref, x_ref, mk_ref, mv_ref, w_hbm, fn_ref, o_ref, y_scr, w_ref,
                stage, sem, *, final):
    @pl.when(pl.program_id(0) == 0)
    def _():
        _load_weight_bf16(w_hbm, w_ref, stage, sem, stage.shape[2])

    dm = MEM_HEAD_DIM
    scale = dm ** -0.5
    out = x_ref[...]
    for t in range(MIX_WIDTH // COL_TILE):
        lo, hi = t * COL_TILE, (t + 1) * COL_TILE
        y_t = (mix_ref[:, lo:hi].astype(F32) * _silu(z_ref[:, lo:hi])).astype(BF16)
        out = out + jnp.dot(y_t, w_ref[lo:hi, :], preferred_element_type=F32)
    for h in range(MEM_HEADS):
        lo, hi = h * dm, (h + 1) * dm
        s = lax.dot_general(qm_ref[:, lo:hi], mk_ref[:, lo:hi], _NT,
                            preferred_element_type=F32) * scale
        p = jnp.exp(s - jnp.max(s, axis=-1, keepdims=True))
        l = jnp.sum(p, axis=-1, keepdims=True)
        o = jnp.dot(p.astype(BF16), mv_ref[:, lo:hi], preferred_element_type=F32) / l
        y_scr[:, lo:hi] = (o * _silu(z_ref[:, MIX_WIDTH + lo:MIX_WIDTH + hi])).astype(BF16)
    out = out + jnp.dot(y_scr[...], w_ref[MIX_WIDTH:, :], preferred_element_type=F32)
    if final:
        ms = jnp.mean(out * out, axis=-1, keepdims=True)
        out = (out * lax.rsqrt(ms + EPS)) * fn_ref[...]
    o_ref[...] = out


def _out_proj(mix, ob, qm_blk, z, x2d, mkv, w_out, final_norm, batch, seq, tm, final):
    m, d = x2d.shape
    per_b = seq // tm
    row = lambda i: (i, 0)
    return pl.pallas_call(
        functools.partial(_out_kernel, final=final),
        grid=(m // tm,),
        in_specs=[
            pl.BlockSpec((tm, MIX_WIDTH), row),
            pl.BlockSpec((tm, MEM_WIDTH), lambda i: (i, qm_blk)),
            pl.BlockSpec((tm, BRANCH_WIDTH), row),
            pl.BlockSpec((tm, d), row),
            pl.BlockSpec((N_MEM, MEM_WIDTH), lambda i: (i // per_b, 0)),
            pl.BlockSpec((N_MEM, MEM_WIDTH), lambda i: (i // per_b, 1)),
            pl.BlockSpec(memory_space=pl.ANY),
            pl.BlockSpec((1, d), lambda i: (0, 0)),
        ],
        out_specs=pl.BlockSpec((tm, d), row),
        out_shape=jax.ShapeDtypeStruct((m, d), F32),
        scratch_shapes=[pltpu.VMEM((tm, MEM_WIDTH), BF16),
                        pltpu.VMEM((BRANCH_WIDTH, d), BF16),
                        pltpu.VMEM((2, BRANCH_WIDTH, _weight_chunk(d)), F32),
                        pltpu.SemaphoreType.DMA((2,))],
        compiler_params=_cparams(1),
        name="out_proj",
    )(mix, ob, z, x2d, mkv, mkv, w_out, final_norm.reshape(1, d))


def _in_proj_plan(mixer):
    n_qm = MEM_WIDTH // LANES
    n_z = BRANCH_WIDTH // LANES
    if mixer == 0:
        n_q = SWA_Q_HEADS * SWA_HEAD_DIM // LANES
        plan = [("RS", gq) for gq in range(n_q)]
        plan += [("R", n_q + n_qm), ("Rh", n_q + n_qm + 1), ("P", n_q + n_qm + 2)]
        plan += [("P", n_q + j) for j in range(n_qm)]
    elif mixer == 1:
        n_h = MIX_WIDTH // LANES
        kinds = ["RS"] * n_h + ["R"] * n_h + ["P"] * (n_h + n_qm)
        plan = [(kind, j) for j, kind in enumerate(kinds)]
    else:
        n_qk = RET_HEADS * RET_QK_DIM // LANES
        kinds = ["R"] * n_qk + ["RS"] * n_qk + ["P"] * (MIX_WIDTH // LANES + n_qm)
        plan = [(kind, j) for j, kind in enumerate(kinds)]
    return plan + [("P", None)] * n_z


def kernel(x, mem, positions, mem_norm, w_mem_kv, norm_0, w_in_0, sinks_0, w_out_0, norm_1, w_in_1,
           w_out_1, norm_2, w_in_2, w_out_2, norm_3, w_in_3, sinks_3, w_out_3, final_norm):
    batch, seq, d = x.shape
    m = batch * seq
    layers = [(norm_0, w_in_0, w_out_0, sinks_0), (norm_1, w_in_1, w_out_1, None),
              (norm_2, w_in_2, w_out_2, None), (norm_3, w_in_3, w_out_3, sinks_3)]

    mkv = _mem_kv(mem.reshape(batch * N_MEM, d), mem_norm, w_mem_kv.astype(BF16))

    pos_col = positions.astype(F32).reshape(m, 1)
    rope = (
        (_rope_lane_rows(SWA_HEAD_DIM, SWA_HEAD_DIM // ROPE_FRACTION, ROPE_THETA),
         SWA_HEAD_DIM // ROPE_FRACTION // 2),
        (_rope_lane_rows(MOBA_HEAD_DIM, MOBA_HEAD_DIM // ROPE_FRACTION, ROPE_THETA),
         MOBA_HEAD_DIM // ROPE_FRACTION // 2),
        (_rope_lane_rows(RET_QK_DIM, RET_QK_DIM, RET_THETA), RET_QK_DIM // 2),
    )

    tm_in = min(256, m)
    tm_out = min(512, seq)
    h = x.reshape(m, d)
    n_layers = len(layers)
    for li, (g, w_in, w_out, sinks) in enumerate(layers):
        mixer = li % N_MIXERS
        lane_rows, half = rope[mixer]
        rs_scale = (SWA_Q_SCALE, MOBA_Q_SCALE, RET_QK_DIM ** -0.5)[mixer]
        ob, z = _in_proj(h, g, w_in, pos_col, lane_rows, _in_proj_plan(mixer), half,
                         rs_scale, tm_in)
        if mixer == 0:
            mix = _swa(ob, sinks, batch, seq)
            qm_blk = SWA_Q_HEADS * SWA_HEAD_DIM // MEM_WIDTH
        elif mixer == 1:
            mix = _moba(ob, batch, seq, cb=min(4, seq // MOBA_BLOCK), qb=2)
            qm_blk = 3 * MIX_WIDTH // MEM_WIDTH
        else:
            mix = _ret(ob, batch, seq, seg=min(1024, seq))
            qm_blk = (2 * RET_HEADS * RET_QK_DIM + MIX_WIDTH) // MEM_WIDTH
        h = _out_proj(mix, ob, qm_blk, z, h, mkv, w_out, final_norm, batch, seq,
                      tm_out, final=(li == n_layers - 1))
    return h.reshape(batch, seq, d)
```

```python
import functools
import math

import jax
import jax.numpy as jnp
from jax import lax
from jax.experimental import pallas as pl
from jax.experimental.pallas import tpu as pltpu

F32 = jnp.float32
BF16 = jnp.bfloat16

D_MODEL = 2048
N_MEM = 256
N_MIXERS = 3
BRANCH_WIDTH = D_MODEL
MEM_HEADS = 4
MEM_HEAD_DIM = 128
MEM_WIDTH = MEM_HEADS * MEM_HEAD_DIM
MIX_WIDTH = BRANCH_WIDTH - MEM_WIDTH

SWA_HEAD_DIM = 64
SWA_Q_HEADS = MIX_WIDTH // SWA_HEAD_DIM
SWA_KV_HEADS = SWA_Q_HEADS // 8
SWA_WINDOW = 128

MOBA_HEAD_DIM = 128
MOBA_HEADS = MIX_WIDTH // MOBA_HEAD_DIM
MOBA_BLOCK = 256
MOBA_TOPK = 3

RET_HEADS = 6
RET_V_DIM = MIX_WIDTH // RET_HEADS
RET_QK_DIM = RET_V_DIM // 2
RET_CHUNK = 128
RET_THETA = 10000.0

ROPE_THETA = 500000.0
ROPE_FRACTION = 4
EPS = 1e-6

LANES = 128
COL_TILE = 512
NEG = -1e30
VMEM_LIMIT = 56 * 1024 * 1024

_NT = (((1,), (1,)), ((), ()))
_TN = (((0,), (0,)), ((), ()))


def _cparams(n_axes):
    return pltpu.CompilerParams(dimension_semantics=("arbitrary",) * n_axes,
                                vmem_limit_bytes=VMEM_LIMIT)


def _silu(z):
    return z * (1.0 / (1.0 + jnp.exp(-z)))


def _weight_stream(w_hbm, w_scr, stage, sem, slices, axis):
    def window(ref, sl, lead=()):
        idx = (sl, slice(None)) if axis == 0 else (slice(None), sl)
        return ref.at[lead + idx]

    def copy(i):
        sl = slices[i]
        local = slice(0, sl.stop - sl.start)
        return pltpu.make_async_copy(window(w_hbm, sl), window(stage, local, (i % 2,)),
                                     sem.at[i % 2])

    def start():
        copy(0).start()

    def fetch(i):
        if i + 1 < len(slices):
            copy(i + 1).start()
        copy(i).wait()
        sl = slices[i]
        local = slice(0, sl.stop - sl.start)
        idx = (sl, slice(None)) if axis == 0 else (slice(None), sl)
        loc = (local, slice(None)) if axis == 0 else (slice(None), local)
        w_scr[idx] = stage[(i % 2,) + loc].astype(BF16)

    return start, fetch


def _in_proj_kernel(x_ref, g_ref, w_hbm, pos_ref, lane_ref, ob_ref, oz_ref, h_scr, w_ref, stage,
                    sem, *, plan, half, rs_scale):
    gpt = COL_TILE // LANES
    n_grp = len(plan)
    tiles = [(g0, min(g0 + gpt, n_grp)) for g0 in range(0, n_grp, gpt)]
    start, fetch = _weight_stream(w_hbm, w_ref, stage, sem,
                                  [slice(g0 * LANES, g1 * LANES) for g0, g1 in tiles], axis=1)

    def body(first_step):
        if first_step:
            start()
        x = x_ref[...]
        ms = jnp.mean(x * x, axis=-1, keepdims=True)
        h_scr[...] = ((x * lax.rsqrt(ms + EPS)) * g_ref[...]).astype(BF16)

        tables = {}

        def rope(a, pat):
            if 0 not in tables:
                ang = pos_ref[...] * lane_ref[0:1, :]
                sn = jnp.sin(ang)
                tables[0] = (jnp.cos(ang), sn * lane_ref[1:2, :], sn * lane_ref[2:3, :])
            if pat not in tables:
                first = lax.broadcasted_iota(jnp.int32, tables[0][0].shape, 1) < LANES // 2
                c0, sp0, sm0 = tables[0]
                tables[pat] = (jnp.where(first, c0, 1.0), jnp.where(first, sp0, 0.0),
                               jnp.where(first, sm0, 0.0))
            c, s_plus, s_minus = tables[pat]
            out = a * c + pltpu.roll(a, half, 1) * s_plus
            if 2 * half != LANES:
                out = out + pltpu.roll(a, LANES - half, 1) * s_minus
            return out

        for ti, (g0, g1) in enumerate(tiles):
            if first_step:
                fetch(ti)
            acc = jnp.dot(h_scr[...], w_ref[:, g0 * LANES:g1 * LANES],
                          preferred_element_type=F32)
            for gi in range(g0, g1):
                kind, dest = plan[gi]
                a = acc[:, (gi - g0) * LANES:(gi - g0 + 1) * LANES]
                if dest is None:
                    zc = (gi - (n_grp - oz_ref.shape[1] // LANES)) * LANES
                    oz_ref[:, zc:zc + LANES] = a
                    continue
                if kind in ("R", "RS", "Rh"):
                    a = rope(a, 1 if kind == "Rh" else 0)
                if kind == "RS":
                    a = a * rs_scale
                ob_ref[:, dest * LANES:(dest + 1) * LANES] = a.astype(BF16)

    @pl.when(pl.program_id(0) == 0)
    def _():
        body(True)

    @pl.when(pl.program_id(0) != 0)
    def _():
        body(False)


def _rope_lane_rows(head_dim, rot_dim, theta):
    half = rot_dim // 2
    inv = theta ** (-jnp.arange(0, rot_dim, 2, dtype=F32) / rot_dim)
    rest = head_dim - rot_dim
    z_half, z_rest = jnp.zeros((half,), F32), jnp.zeros((rest,), F32)
    ones = jnp.ones((half,), F32)
    inv_h = jnp.concatenate([inv, inv, z_rest])
    plus_h = jnp.concatenate([z_half, ones, z_rest])
    minus_h = jnp.concatenate([-ones, z_half, z_rest])
    if 2 * half == LANES:
        plus_h, minus_h = plus_h + minus_h, jnp.zeros_like(minus_h)
    rep = LANES // head_dim
    return jnp.stack([jnp.tile(r, rep) for r in (inv_h, plus_h, minus_h)])


def _in_proj(x2d, g, w_in, pos_col, lane_rows, plan, half, rs_scale, tm):
    m, d = x2d.shape
    n_tot = w_in.shape[1]
    n_bf = (max(dest for _, dest in plan if dest is not None) + 1) * LANES
    row = lambda i: (i, 0)
    fixed = lambda i: (0, 0)
    return pl.pallas_call(
        functools.partial(_in_proj_kernel, plan=tuple(plan), half=half, rs_scale=rs_scale),
        grid=(m // tm,),
        in_specs=[
            pl.BlockSpec((tm, d), row),
            pl.BlockSpec((1, d), fixed),
            pl.BlockSpec(memory_space=pl.ANY),
            pl.BlockSpec((tm, 1), row),
            pl.BlockSpec(lane_rows.shape, fixed),
        ],
        out_specs=[pl.BlockSpec((tm, n_bf), row), pl.BlockSpec((tm, BRANCH_WIDTH), row)],
        out_shape=[jax.ShapeDtypeStruct((m, n_bf), BF16),
                   jax.ShapeDtypeStruct((m, BRANCH_WIDTH), F32)],
        scratch_shapes=[pltpu.VMEM((tm, d), BF16),
                        pltpu.VMEM((d, n_tot), BF16),
                        pltpu.VMEM((2, d, COL_TILE), F32),
                        pltpu.SemaphoreType.DMA((2,))],
        compiler_params=_cparams(1),
        name="in_proj",
    )(x2d, g.reshape(1, d), w_in, pos_col, lane_rows)


def _mem_kv_kernel(x_ref, g_ref, w_ref, o_ref):
    x = x_ref[...]
    ms = jnp.mean(x * x, axis=-1, keepdims=True)
    h = ((x * lax.rsqrt(ms + EPS)) * g_ref[...]).astype(BF16)
    o_ref[...] = jnp.dot(h, w_ref[...], preferred_element_type=F32).astype(BF16)


def _mem_kv(mem2d, g, w_bf16):
    m, d = mem2d.shape
    n = w_bf16.shape[1]
    tm = min(m, 256)
    return pl.pallas_call(
        _mem_kv_kernel,
        grid=(m // tm,),
        in_specs=[
            pl.BlockSpec((tm, d), lambda i: (i, 0)),
            pl.BlockSpec((1, d), lambda i: (0, 0)),
            pl.BlockSpec((d, n), lambda i: (0, 0)),
        ],
        out_specs=pl.BlockSpec((tm, n), lambda i: (i, 0)),
        out_shape=jax.ShapeDtypeStruct((m, n), BF16),
        compiler_params=_cparams(1),
        name="mem_kv",
    )(mem2d, g.reshape(1, d), w_bf16)


SWA_Q_SCALE = SWA_HEAD_DIM ** -0.5 * math.log2(math.e)


def _swa_kernel(sink_ref, q_ref, ka_ref, kb_ref, kc_ref, o_ref, vt, s_scr, *, seq):
    w = SWA_WINDOW
    dh = SWA_HEAD_DIM
    g_per = SWA_Q_HEADS // SWA_KV_HEADS
    t = pl.program_id(1)

    @pl.when(t == 0)
    def _():
        for j in range(seq // w):
            rows = slice(j * w, (j + 1) * w)
            both = jnp.concatenate([kb_ref[rows, :], kc_ref[rows, :]], axis=1)
            vt[j] = both.astype(F32).T.astype(BF16)

    j0 = jnp.maximum(t - 1, 0)
    r0 = pl.multiple_of(j0 * w, w)
    k_wins = (ka_ref[pl.ds(r0, 2 * w), :], kb_ref[pl.ds(r0, 2 * w), :])
    vt_win = jnp.concatenate([vt[j0], vt[j0 + 1]], axis=1)
    q_t = q_ref[...].astype(F32).T.astype(BF16)
    kpos = r0 + lax.broadcasted_iota(jnp.int32, (2 * w, w), 0)
    qpos = t * w + lax.broadcasted_iota(jnp.int32, (2 * w, w), 1)
    bias = jnp.where((kpos <= qpos) & (kpos > qpos - w), 0.0, NEG)
    bias = jnp.concatenate([bias] * g_per, axis=1)
    zeros = jnp.zeros((dh, g_per * w), BF16)
    for h in range(SWA_KV_HEADS):
        q_grp = jnp.concatenate(
            [q_t[(h * g_per + g) * dh:(h * g_per + g + 1) * dh, :] for g in range(g_per)], axis=1)
        q_pad = jnp.concatenate([q_grp, zeros] if h % 2 == 0 else [zeros, q_grp], axis=0)
        s_scr[h] = jnp.dot(k_wins[h // 2], q_pad, preferred_element_type=F32) + bias
    for h in range(SWA_KV_HEADS):
        s = s_scr[h]
        sink = sink_ref[h:h + 1, :]
        mx = jnp.maximum(jnp.max(s, axis=0, keepdims=True), sink)
        p = jnp.exp2(s - mx)
        denom = jnp.sum(p, axis=0, keepdims=True) + jnp.exp2(sink - mx)
        o_t = jnp.dot(vt_win[(h + 1) * dh:(h + 2) * dh, :], p.astype(BF16),
                      preferred_element_type=F32) / denom
        for pair in range(g_per // 2):
            two = jnp.concatenate([o_t[:, (2 * pair) * w:(2 * pair + 1) * w],
                                   o_t[:, (2 * pair + 1) * w:(2 * pair + 2) * w]], axis=0)
            c0 = (h * g_per + 2 * pair) * dh
            o_ref[:, c0:c0 + 2 * dh] = two.T.astype(BF16)


def _swa(ob, sinks, batch, seq):
    m = batch * seq
    w = SWA_WINDOW
    nq = SWA_Q_HEADS * SWA_HEAD_DIM
    g_per = SWA_Q_HEADS // SWA_KV_HEADS
    n_t = seq // w
    kv_grp = (nq + MEM_WIDTH) // LANES
    sink_rows = jnp.repeat(sinks.astype(F32) * math.log2(math.e), w).reshape(SWA_KV_HEADS, g_per * w)
    kv_spec = lambda j: pl.BlockSpec((seq, LANES), lambda b, t: (b, kv_grp + j))
    return pl.pallas_call(
        functools.partial(_swa_kernel, seq=seq),
        grid=(batch, n_t),
        in_specs=[
            pl.BlockSpec((SWA_KV_HEADS, g_per * w), lambda b, t: (0, 0)),
            pl.BlockSpec((w, nq), lambda b, t: (b * n_t + t, 0)),
            kv_spec(0), kv_spec(1), kv_spec(2),
        ],
        out_specs=pl.BlockSpec((w, nq), lambda b, t: (b * n_t + t, 0)),
        out_shape=jax.ShapeDtypeStruct((m, nq), BF16),
        scratch_shapes=[pltpu.VMEM((seq // w, 2 * LANES, w), BF16),
                        pltpu.VMEM((SWA_KV_HEADS, 2 * w, g_per * w), F32)],
        compiler_params=_cparams(2),
        name="swa",
    )(sink_rows, ob, ob, ob, ob)


MOBA_BIAS_ROWS = 16
MOBA_Q_SCALE = MOBA_HEAD_DIM ** -0.5 * math.log2(math.e)


def _moba_kernel(q_ref, k_ref, v_ref, o_ref, kaug, vt, kmean, qaug, s_a, s_b, *, seq, cb, qb):
    blk = MOBA_BLOCK
    dh = MOBA_HEAD_DIM
    nblk = seq // blk
    kc = cb * blk
    wq = qb * blk
    nbr = MOBA_BIAS_ROWS
    i = pl.program_id(2)

    @pl.when(i == 0)
    def _():
        k = k_ref[...]
        kaug[:, :dh] = k
        row_blk = lax.broadcasted_iota(jnp.int32, (seq, LANES), 0) // blk
        lane = lax.broadcasted_iota(jnp.int32, (seq, LANES), 1)
        kaug[:, dh:] = (row_blk == lane).astype(BF16)
        kmean[...] = jnp.zeros_like(kmean)
        kmean[:nblk, :] = jnp.mean(k.astype(F32).reshape(nblk, blk, dh), axis=1)
        for c in range(seq // kc):
            vt[:, c * kc:(c + 1) * kc] = v_ref[c * kc:(c + 1) * kc, :].astype(F32).T.astype(BF16)
        qaug[dh + nbr:, :] = jnp.zeros((dh - nbr, wq), BF16)

    q_t = q_ref[...].astype(F32).T.astype(BF16)
    gate = jnp.dot(kmean[...].astype(BF16), q_t, preferred_element_type=F32)
    row = lax.broadcasted_iota(jnp.int32, (nbr, wq), 0)
    rowf = row.astype(F32)
    own = i * qb + lax.broadcasted_iota(jnp.int32, (nbr, wq), 1) // blk
    past = row < own
    g = jnp.where(past, gate, -jnp.inf)
    sel = row == own
    for _ in range(min(MOBA_TOPK, nblk - 1)):
        mx = jnp.max(g, axis=0, keepdims=True)
        idx = jnp.min(jnp.where(g == mx, rowf, float(nbr)), axis=0, keepdims=True)
        hit = rowf == idx
        sel = sel | (hit & past)
        g = jnp.where(hit, -jnp.inf, g)
    qaug[:dh, :] = q_t
    qaug[dh:dh + nbr, :] = jnp.where(sel, 0.0, NEG).astype(BF16)

    c_own = (i * qb) // cb
    qpos = i * wq + lax.broadcasted_iota(jnp.int32, (kc, wq), 1)
    for case in range(nblk // cb):
        @pl.when(c_own == case)
        def _(case=case):
            order = [case] + list(range(case))
            s_bufs = (s_a, s_b)

            def stage_scores(idx):
                c = order[idx]
                s = jnp.dot(kaug[c * kc:(c + 1) * kc, :], qaug[...],
                            preferred_element_type=F32)
                if c == case:
                    kpos = c * kc + lax.broadcasted_iota(jnp.int32, (kc, wq), 0)
                    s = jnp.where(kpos <= qpos, s, NEG)
                s_bufs[idx % 2][...] = s

            stage_scores(0)
            m_col = l_col = acc = None
            for idx, c in enumerate(order):
                if idx + 1 < len(order):
                    stage_scores(idx + 1)
                rows = slice(c * kc, (c + 1) * kc)
                s = s_bufs[idx % 2][...]
                m_c = jnp.max(s, axis=0, keepdims=True)
                if m_col is None:
                    m_col = m_c
                    p = jnp.exp2(s - m_col)
                    l_col = jnp.sum(p, axis=0, keepdims=True)
                    acc = jnp.dot(vt[:, rows], p.astype(BF16), preferred_element_type=F32)
                else:
                    m_new = jnp.maximum(m_col, m_c)
                    alpha = jnp.exp2(m_col - m_new)
                    p = jnp.exp2(s - m_new)
                    l_col = alpha * l_col + jnp.sum(p, axis=0, keepdims=True)
                    acc = alpha * acc + jnp.dot(vt[:, rows], p.astype(BF16),
                                                preferred_element_type=F32)
                    m_col = m_new
            o_ref[...] = (acc / l_col).T.astype(BF16)


def _moba(ob, batch, seq, cb, qb):
    m = batch * seq
    blk = MOBA_BLOCK
    dh = MOBA_HEAD_DIM
    nh = MOBA_HEADS
    n_q = seq // (qb * blk)
    wq = qb * blk
    return pl.pallas_call(
        functools.partial(_moba_kernel, seq=seq, cb=cb, qb=qb),
        grid=(batch, nh, n_q),
        in_specs=[
            pl.BlockSpec((wq, dh), lambda b, h, i: (b * n_q + i, h)),
            pl.BlockSpec((seq, dh), lambda b, h, i: (b, nh + h)),
            pl.BlockSpec((seq, dh), lambda b, h, i: (b, 2 * nh + h)),
        ],
        out_specs=pl.BlockSpec((wq, dh), lambda b, h, i: (b * n_q + i, h)),
        out_shape=jax.ShapeDtypeStruct((m, nh * dh), BF16),
        scratch_shapes=[pltpu.VMEM((seq, 2 * dh), BF16),
                        pltpu.VMEM((dh, seq), BF16),
                        pltpu.VMEM((MOBA_BIAS_ROWS, dh), F32),
                        pltpu.VMEM((2 * dh, wq), BF16),
                        pltpu.VMEM((cb * blk, wq), F32),
                        pltpu.VMEM((cb * blk, wq), F32)],
        compiler_params=_cparams(3),
        name="moba",
    )(ob, ob, ob)


def _ret_kernel(qk_ref, v_ref, dec_ref, xi_ref, zeta_ref, o_ref, r_scr, *, n_chunk, g_chunk):
    t = RET_CHUNK
    nh, dk, dv = RET_HEADS, RET_QK_DIM, RET_V_DIM

    @pl.when(pl.program_id(1) == 0)
    def _():
        r_scr[...] = jnp.zeros_like(r_scr)

    def body(c, carry):
        r0 = pl.multiple_of(c * t, t)
        for h in range(nh):
            q = qk_ref[pl.ds(r0, t), h * dk:(h + 1) * dk]
            k = qk_ref[pl.ds(r0, t), (nh + h) * dk:(nh + h + 1) * dk]
            v = v_ref[pl.ds(r0, t), h * dv:(h + 1) * dv]
            s = lax.dot_general(q, k, _NT, preferred_element_type=F32) * dec_ref[h]
            inner = jnp.dot(s.astype(BF16), v, preferred_element_type=F32)
            r_prev = r_scr[h]
            q_x = (q.astype(F32) * xi_ref[h]).astype(BF16)
            cross = jnp.dot(q_x, r_prev.astype(BF16), preferred_element_type=F32)
            o = inner + cross
            o = o * lax.rsqrt(jnp.mean(o * o, axis=-1, keepdims=True) + EPS)
            o_ref[pl.ds(r0, t), h * dv:(h + 1) * dv] = o.astype(BF16)
            k_z = (k.astype(F32) * zeta_ref[h]).astype(BF16)
            kv = lax.dot_general(k_z, v, _TN, preferred_element_type=F32)
            r_scr[h] = g_chunk[h] * r_prev + kv
        return carry

    lax.fori_loop(0, n_chunk, body, 0)


def _ret_constants():
    nh, t, dk = RET_HEADS, RET_CHUNK, RET_QK_DIM
    lin = [math.log(1.0 / 32) + (math.log(1.0 / 512) - math.log(1.0 / 32)) * h / (nh - 1)
           for h in range(nh)]
    log_g = [math.log1p(-math.exp(v)) for v in lin]
    i = jnp.arange(t, dtype=F32)
    lg = jnp.asarray(log_g, F32)
    diff = i[:, None] - i[None, :]
    decay = jnp.where(diff >= 0, jnp.exp(jnp.maximum(diff, 0.0)[None] * lg[:, None, None]), 0.0)
    xi = jnp.exp((i + 1)[None, :] * lg[:, None])
    zeta = jnp.exp((t - 1 - i)[None, :] * lg[:, None])
    xi_t = jnp.broadcast_to(xi[:, :, None], (nh, t, dk))
    zeta_t = jnp.broadcast_to(zeta[:, :, None], (nh, t, dk))
    g_chunk = tuple(math.exp(t * v) for v in log_g)
    return decay, xi_t, zeta_t, g_chunk


def _ret(ob, batch, seq, seg):
    m = batch * seq
    nh, dk, dv, t = RET_HEADS, RET_QK_DIM, RET_V_DIM, RET_CHUNK
    n_seg = seq // seg
    decay, xi_t, zeta_t, g_chunk = _ret_constants()
    wqk = 2 * nh * dk
    wv = nh * dv
    const = lambda b, s: (0, 0, 0)
    return pl.pallas_call(
        functools.partial(_ret_kernel, n_chunk=seg // t, g_chunk=g_chunk),
        grid=(batch, n_seg),
        in_specs=[
            pl.BlockSpec((seg, wqk), lambda b, s: (b * n_seg + s, 0)),
            pl.BlockSpec((seg, wv), lambda b, s: (b * n_seg + s, wqk // wv)),
            pl.BlockSpec((nh, t, t), const),
            pl.BlockSpec((nh, t, dk), const),
            pl.BlockSpec((nh, t, dk), const),
        ],
        out_specs=pl.BlockSpec((seg, wv), lambda b, s: (b * n_seg + s, 0)),
        out_shape=jax.ShapeDtypeStruct((m, wv), BF16),
        scratch_shapes=[pltpu.VMEM((nh, dk, dv), F32)],
        compiler_params=_cparams(2),
        name="retention",
    )(ob, ob, decay, xi_t, zeta_t)


def _out_kernel(mix_ref, qm_ref, z_ref, x_ref, mk_ref, mv_ref, w_hbm, fn_ref, o_ref, y_scr, w_ref,
                stage, sem, *, final):
    dm = MEM_HEAD_DIM
    scale = dm ** -0.5
    n_slices = BRANCH_WIDTH // COL_TILE
    rows = stage.shape[1]
    per = COL_TILE // rows
    start, fetch_piece = _weight_stream(
        w_hbm, w_ref, stage, sem,
        [slice(j * rows, (j + 1) * rows) for j in range(BRANCH_WIDTH // rows)], axis=0)

    def fetch(t):
        for j in range(per):
            fetch_piece(t * per + j)

    def body(first_step):
        if first_step:
            start()
        out = x_ref[...]
        for t in range(MIX_WIDTH // COL_TILE):
            lo, hi = t * COL_TILE, (t + 1) * COL_TILE
            y_t = (mix_ref[:, lo:hi].astype(F32) * _silu(z_ref[:, lo:hi])).astype(BF16)
            if first_step:
                fetch(t)
            out = out + jnp.dot(y_t, w_ref[lo:hi, :], preferred_element_type=F32)
        for h in range(MEM_HEADS):
            lo, hi = h * dm, (h + 1) * dm
            s = lax.dot_general(qm_ref[:, lo:hi], mk_ref[:, lo:hi], _NT,
                                preferred_element_type=F32) * scale
            p = jnp.exp(s - jnp.max(s, axis=-1, keepdims=True))
            l = jnp.sum(p, axis=-1, keepdims=True)
            o = jnp.dot(p.astype(BF16), mv_ref[:, lo:hi], preferred_element_type=F32) / l
            y_scr[:, lo:hi] = (o * _silu(z_ref[:, MIX_WIDTH + lo:MIX_WIDTH + hi])).astype(BF16)
        if first_step:
            fetch(n_slices - 1)
        out = out + jnp.dot(y_scr[...], w_ref[MIX_WIDTH:, :], preferred_element_type=F32)
        if final:
            ms = jnp.mean(out * out, axis=-1, keepdims=True)
            out = (out * lax.rsqrt(ms + EPS)) * fn_ref[...]
        o_ref[...] = out

    @pl.when(pl.program_id(0) == 0)
    def _():
        body(True)

    @pl.when(pl.program_id(0) != 0)
    def _():
        body(False)


def _out_proj(mix, ob, qm_blk, z, x2d, mkv, w_out, final_norm, batch, seq, tm, final):
    m, d = x2d.shape
    per_b = seq // tm
    row = lambda i: (i, 0)
    return pl.pallas_call(
        functools.partial(_out_kernel, final=final),
        grid=(m // tm,),
        in_specs=[
            pl.BlockSpec((tm, MIX_WIDTH), row),
            pl.BlockSpec((tm, MEM_WIDTH), lambda i: (i, qm_blk)),
            pl.BlockSpec((tm, BRANCH_WIDTH), row),
            pl.BlockSpec((tm, d), row),
            pl.BlockSpec((N_MEM, MEM_WIDTH), lambda i: (i // per_b, 0)),
            pl.BlockSpec((N_MEM, MEM_WIDTH), lambda i: (i // per_b, 1)),
            pl.BlockSpec(memory_space=pl.ANY),
            pl.BlockSpec((1, d), lambda i: (0, 0)),
        ],
        out_specs=pl.BlockSpec((tm, d), row),
        out_shape=jax.ShapeDtypeStruct((m, d), F32),
        scratch_shapes=[pltpu.VMEM((tm, MEM_WIDTH), BF16),
                        pltpu.VMEM((BRANCH_WIDTH, d), BF16),
                        pltpu.VMEM((2, COL_TILE // 2, d), F32),
                        pltpu.SemaphoreType.DMA((2,))],
        compiler_params=_cparams(1),
        name="out_proj",
    )(mix, ob, z, x2d, mkv, mkv, w_out, final_norm.reshape(1, d))


def _in_proj_plan(mixer):
    n_qm = MEM_WIDTH // LANES
    n_z = BRANCH_WIDTH // LANES
    if mixer == 0:
        n_q = SWA_Q_HEADS * SWA_HEAD_DIM // LANES
        plan = [("RS", gq) for gq in range(n_q)]
        plan += [("R", n_q + n_qm), ("Rh", n_q + n_qm + 1), ("P", n_q + n_qm + 2)]
        plan += [("P", n_q + j) for j in range(n_qm)]
    elif mixer == 1:
        n_h = MIX_WIDTH // LANES
        kinds = ["RS"] * n_h + ["R"] * n_h + ["P"] * (n_h + n_qm)
        plan = [(kind, j) for j, kind in enumerate(kinds)]
    else:
        n_qk = RET_HEADS * RET_QK_DIM // LANES
        kinds = ["R"] * n_qk + ["RS"] * n_qk + ["P"] * (MIX_WIDTH // LANES + n_qm)
        plan = [(kind, j) for j, kind in enumerate(kinds)]
    return plan + [("P", None)] * n_z


def kernel(x, mem, positions, mem_norm, w_mem_kv, norm_0, w_in_0, sinks_0, w_out_0, norm_1, w_in_1,
           w_out_1, norm_2, w_in_2, w_out_2, norm_3, w_in_3, sinks_3, w_out_3, final_norm):
    batch, seq, d = x.shape
    m = batch * seq
    layers = [(norm_0, w_in_0, w_out_0, sinks_0), (norm_1, w_in_1, w_out_1, None),
              (norm_2, w_in_2, w_out_2, None), (norm_3, w_in_3, w_out_3, sinks_3)]

    mkv = _mem_kv(mem.reshape(batch * N_MEM, d), mem_norm, w_mem_kv.astype(BF16))

    pos_col = positions.astype(F32).reshape(m, 1)
    rope = (
        (_rope_lane_rows(SWA_HEAD_DIM, SWA_HEAD_DIM // ROPE_FRACTION, ROPE_THETA),
         SWA_HEAD_DIM // ROPE_FRACTION // 2),
        (_rope_lane_rows(MOBA_HEAD_DIM, MOBA_HEAD_DIM // ROPE_FRACTION, ROPE_THETA),
         MOBA_HEAD_DIM // ROPE_FRACTION // 2),
        (_rope_lane_rows(RET_QK_DIM, RET_QK_DIM, RET_THETA), RET_QK_DIM // 2),
    )

    tm_in = min(256, m)
    tm_out = min(512, seq)
    h = x.reshape(m, d)
    n_layers = len(layers)
    for li, (g, w_in, w_out, sinks) in enumerate(layers):
        mixer = li % N_MIXERS
        lane_rows, half = rope[mixer]
        rs_scale = (SWA_Q_SCALE, MOBA_Q_SCALE, RET_QK_DIM ** -0.5)[mixer]
        ob, z = _in_proj(h, g, w_in, pos_col, lane_rows, _in_proj_plan(mixer), half,
                         rs_scale, tm_in)
        if mixer == 0:
            mix = _swa(ob, sinks, batch, seq)
            qm_blk = SWA_Q_HEADS * SWA_HEAD_DIM // MEM_WIDTH
        elif mixer == 1:
            mix = _moba(ob, batch, seq, cb=min(4, seq // MOBA_BLOCK), qb=4)
            qm_blk = 3 * MIX_WIDTH // MEM_WIDTH
        else:
            mix = _ret(ob, batch, seq, seg=min(1024, seq))
            qm_blk = (2 * RET_HEADS * RET_QK_DIM + MIX_WIDTH) // MEM_WIDTH
        h = _out_proj(mix, ob, qm_blk, z, h, mkv, w_out, final_norm, batch, seq,
                      tm_out, final=(li == n_layers - 1))
    return h.reshape(batch, seq, d)
```

```python
import functools
import math

import jax
import jax.numpy as jnp
from jax import lax
from jax.experimental import pallas as pl
from jax.experimental.pallas import tpu as pltpu

F32 = jnp.float32
BF16 = jnp.bfloat16

D_MODEL = 2048
N_MEM = 256
N_MIXERS = 3
BRANCH_WIDTH = D_MODEL
MEM_HEADS = 4
MEM_HEAD_DIM = 128
MEM_WIDTH = MEM_HEADS * MEM_HEAD_DIM
MIX_WIDTH = BRANCH_WIDTH - MEM_WIDTH

SWA_HEAD_DIM = 64
SWA_Q_HEADS = MIX_WIDTH // SWA_HEAD_DIM
SWA_KV_HEADS = SWA_Q_HEADS // 8
SWA_WINDOW = 128

MOBA_HEAD_DIM = 128
MOBA_HEADS = MIX_WIDTH // MOBA_HEAD_DIM
MOBA_BLOCK = 256
MOBA_TOPK = 3

RET_HEADS = 6
RET_V_DIM = MIX_WIDTH // RET_HEADS
RET_QK_DIM = RET_V_DIM // 2
RET_CHUNK = 128
RET_THETA = 10000.0

ROPE_THETA = 500000.0
ROPE_FRACTION = 4
EPS = 1e-6

LANES = 128
COL_TILE = 512
NEG = -1e30
VMEM_LIMIT = 56 * 1024 * 1024

_NT = (((1,), (1,)), ((), ()))
_TN = (((0,), (0,)), ((), ()))


def _cparams(n_axes):
    return pltpu.CompilerParams(dimension_semantics=("arbitrary",) * n_axes,
                                vmem_limit_bytes=VMEM_LIMIT)


def _silu(z):
    return z * (1.0 / (1.0 + jnp.exp(-z)))


def _weight_stream(w_hbm, w_scr, stage, sem, slices, axis):
    def window(ref, sl, lead=()):
        idx = (sl, slice(None)) if axis == 0 else (slice(None), sl)
        return ref.at[lead + idx]

    def copy(i):
        sl = slices[i]
        local = slice(0, sl.stop - sl.start)
        return pltpu.make_async_copy(window(w_hbm, sl), window(stage, local, (i % 2,)),
                                     sem.at[i % 2])

    def start():
        copy(0).start()

    def fetch(i):
        if i + 1 < len(slices):
            copy(i + 1).start()
        copy(i).wait()
        sl = slices[i]
        local = slice(0, sl.stop - sl.start)
        idx = (sl, slice(None)) if axis == 0 else (slice(None), sl)
        loc = (local, slice(None)) if axis == 0 else (slice(None), local)
        w_scr[idx] = stage[(i % 2,) + loc].astype(BF16)

    return start, fetch


def _in_proj_kernel(x_ref, g_ref, w_hbm, pos_ref, lane_ref, ob_ref, oz_ref, *rest, plan, half,
                    rs_scale):
    qt_ref = rest[0] if len(rest) == 5 else None
    h_scr, w_ref, stage, sem = rest[-4:]
    gpt = COL_TILE // LANES
    n_grp = len(plan)
    tiles = [(g0, min(g0 + gpt, n_grp)) for g0 in range(0, n_grp, gpt)]
    start, fetch = _weight_stream(w_hbm, w_ref, stage, sem,
                                  [slice(g0 * LANES, g1 * LANES) for g0, g1 in tiles], axis=1)

    def body(first_step):
        if first_step:
            start()
        x = x_ref[...]
        ms = jnp.mean(x * x, axis=-1, keepdims=True)
        h_scr[...] = ((x * lax.rsqrt(ms + EPS)) * g_ref[...]).astype(BF16)

        tables = {}

        def rope(a, pat):
            if 0 not in tables:
                ang = pos_ref[...] * lane_ref[0:1, :]
                sn = jnp.sin(ang)
                tables[0] = (jnp.cos(ang), sn * lane_ref[1:2, :], sn * lane_ref[2:3, :])
            if pat not in tables:
                first = lax.broadcasted_iota(jnp.int32, tables[0][0].shape, 1) < LANES // 2
                c0, sp0, sm0 = tables[0]
                tables[pat] = (jnp.where(first, c0, 1.0), jnp.where(first, sp0, 0.0),
                               jnp.where(first, sm0, 0.0))
            c, s_plus, s_minus = tables[pat]
            out = a * c + pltpu.roll(a, half, 1) * s_plus
            if 2 * half != LANES:
                out = out + pltpu.roll(a, LANES - half, 1) * s_minus
            return out

        for ti, (g0, g1) in enumerate(tiles):
            if first_step:
                fetch(ti)
            acc = jnp.dot(h_scr[...], w_ref[:, g0 * LANES:g1 * LANES],
                          preferred_element_type=F32)
            for gi in range(g0, g1):
                kind, dest = plan[gi]
                a = acc[:, (gi - g0) * LANES:(gi - g0 + 1) * LANES]
                if dest is None:
                    zc = (gi - (n_grp - oz_ref.shape[1] // LANES)) * LANES
                    oz_ref[:, zc:zc + LANES] = a
                    continue
                if kind in ("R", "RS", "RST", "Rh"):
                    a = rope(a, 1 if kind == "Rh" else 0)
                if kind in ("RS", "RST"):
                    a = a * rs_scale
                if kind in ("RST", "PT"):
                    qt_ref[dest * LANES:(dest + 1) * LANES, :] = a.T.astype(BF16)
                else:
                    ob_ref[:, dest * LANES:(dest + 1) * LANES] = a.astype(BF16)

    @pl.when(pl.program_id(0) == 0)
    def _():
        body(True)

    @pl.when(pl.program_id(0) != 0)
    def _():
        body(False)


def _rope_lane_rows(head_dim, rot_dim, theta):
    half = rot_dim // 2
    inv = theta ** (-jnp.arange(0, rot_dim, 2, dtype=F32) / rot_dim)
    rest = head_dim - rot_dim
    z_half, z_rest = jnp.zeros((half,), F32), jnp.zeros((rest,), F32)
    ones = jnp.ones((half,), F32)
    inv_h = jnp.concatenate([inv, inv, z_rest])
    plus_h = jnp.concatenate([z_half, ones, z_rest])
    minus_h = jnp.concatenate([-ones, z_half, z_rest])
    if 2 * half == LANES:
        plus_h, minus_h = plus_h + minus_h, jnp.zeros_like(minus_h)
    rep = LANES // head_dim
    return jnp.stack([jnp.tile(r, rep) for r in (inv_h, plus_h, minus_h)])


def _in_proj(x2d, g, w_in, pos_col, lane_rows, plan, half, rs_scale, tm):
    m, d = x2d.shape
    n_tot = w_in.shape[1]
    n_bf = (max(dest for kind, dest in plan if dest is not None and kind[-1] != "T") + 1) * LANES
    n_qt = sum(kind[-1] == "T" for kind, _ in plan) * LANES
    row = lambda i: (i, 0)
    fixed = lambda i: (0, 0)
    out_specs = [pl.BlockSpec((tm, n_bf), row), pl.BlockSpec((tm, BRANCH_WIDTH), row)]
    out_shape = [jax.ShapeDtypeStruct((m, n_bf), BF16), jax.ShapeDtypeStruct((m, BRANCH_WIDTH), F32)]
    if n_qt:
        out_specs.append(pl.BlockSpec((n_qt, tm), lambda i: (0, i)))
        out_shape.append(jax.ShapeDtypeStruct((n_qt, m), BF16))
    return pl.pallas_call(
        functools.partial(_in_proj_kernel, plan=tuple(plan), half=half, rs_scale=rs_scale),
        grid=(m // tm,),
        in_specs=[
            pl.BlockSpec((tm, d), row),
            pl.BlockSpec((1, d), fixed),
            pl.BlockSpec(memory_space=pl.ANY),
            pl.BlockSpec((tm, 1), row),
            pl.BlockSpec(lane_rows.shape, fixed),
        ],
        out_specs=out_specs,
        out_shape=out_shape,
        scratch_shapes=[pltpu.VMEM((tm, d), BF16),
                        pltpu.VMEM((d, n_tot), BF16),
                        pltpu.VMEM((2, d, COL_TILE), F32),
                        pltpu.SemaphoreType.DMA((2,))],
        compiler_params=_cparams(1),
        name="in_proj",
    )(x2d, g.reshape(1, d), w_in, pos_col, lane_rows)


def _mem_kv_kernel(x_ref, g_ref, w_ref, o_ref):
    x = x_ref[...]
    ms = jnp.mean(x * x, axis=-1, keepdims=True)
    h = ((x * lax.rsqrt(ms + EPS)) * g_ref[...]).astype(BF16)
    o_ref[...] = jnp.dot(h, w_ref[...], preferred_element_type=F32).astype(BF16)


def _mem_kv(mem2d, g, w_bf16):
    m, d = mem2d.shape
    n = w_bf16.shape[1]
    tm = min(m, 256)
    return pl.pallas_call(
        _mem_kv_kernel,
        grid=(m // tm,),
        in_specs=[
            pl.BlockSpec((tm, d), lambda i: (i, 0)),
            pl.BlockSpec((1, d), lambda i: (0, 0)),
            pl.BlockSpec((d, n), lambda i: (0, 0)),
        ],
        out_specs=pl.BlockSpec((tm, n), lambda i: (i, 0)),
        out_shape=jax.ShapeDtypeStruct((m, n), BF16),
        compiler_params=_cparams(1),
        name="mem_kv",
    )(mem2d, g.reshape(1, d), w_bf16)


SWA_Q_SCALE = SWA_HEAD_DIM ** -0.5 * math.log2(math.e)


SWA_SUB = 2


def _swa_kernel(sink_ref, qt_ref, ka_ref, kb_ref, kc_ref, o_ref, vt, s_scr, *, seq):
    w = SWA_WINDOW
    dh = SWA_HEAD_DIM
    g_per = SWA_Q_HEADS // SWA_KV_HEADS
    step = pl.program_id(1)

    @pl.when(step == 0)
    def _():
        for j in range(seq // w):
            rows = slice(j * w, (j + 1) * w)
            both = jnp.concatenate([kb_ref[rows, :], kc_ref[rows, :]], axis=1)
            vt[j] = both.astype(F32).T.astype(BF16)

    zeros = jnp.zeros((dh, g_per * w), BF16)
    vt_wins = []
    for sub in range(SWA_SUB):
        t = step * SWA_SUB + sub
        j0 = jnp.maximum(t - 1, 0)
        r0 = pl.multiple_of(j0 * w, w)
        k_wins = (ka_ref[pl.ds(r0, 2 * w), :], kb_ref[pl.ds(r0, 2 * w), :])
        vt_wins.append(jnp.concatenate([vt[j0], vt[j0 + 1]], axis=1))
        kpos = r0 + lax.broadcasted_iota(jnp.int32, (2 * w, w), 0)
        qpos = t * w + lax.broadcasted_iota(jnp.int32, (2 * w, w), 1)
        bias = jnp.where((kpos <= qpos) & (kpos > qpos - w), 0.0, NEG)
        bias = jnp.concatenate([bias] * g_per, axis=1)
        for h in range(SWA_KV_HEADS):
            q_grp = jnp.concatenate(
                [qt_ref[(h * g_per + g) * dh:(h * g_per + g + 1) * dh, sub * w:(sub + 1) * w]
                 for g in range(g_per)], axis=1)
            q_pad = jnp.concatenate([q_grp, zeros] if h % 2 == 0 else [zeros, q_grp], axis=0)
            s_scr[sub * SWA_KV_HEADS + h] = (
                jnp.dot(k_wins[h // 2], q_pad, preferred_element_type=F32) + bias)
    for sub in range(SWA_SUB):
        for h in range(SWA_KV_HEADS):
            s = s_scr[sub * SWA_KV_HEADS + h]
            sink = sink_ref[h:h + 1, :]
            mx = jnp.maximum(jnp.max(s, axis=0, keepdims=True), sink)
            p = jnp.exp2(s - mx)
            denom = jnp.sum(p, axis=0, keepdims=True) + jnp.exp2(sink - mx)
            o_t = jnp.dot(vt_wins[sub][(h + 1) * dh:(h + 2) * dh, :], p.astype(BF16),
                          preferred_element_type=F32) / denom
            for pair in range(g_per // 2):
                two = jnp.concatenate([o_t[:, (2 * pair) * w:(2 * pair + 1) * w],
                                       o_t[:, (2 * pair + 1) * w:(2 * pair + 2) * w]], axis=0)
                c0 = (h * g_per + 2 * pair) * dh
                o_ref[sub * w:(sub + 1) * w, c0:c0 + 2 * dh] = two.T.astype(BF16)


def _swa(qt, ob, sinks, batch, seq):
    m = batch * seq
    w = SWA_WINDOW
    nq = SWA_Q_HEADS * SWA_HEAD_DIM
    g_per = SWA_Q_HEADS // SWA_KV_HEADS
    n_t = seq // (SWA_SUB * w)
    kv_grp = MEM_WIDTH // LANES
    sink_rows = jnp.repeat(sinks.astype(F32) * math.log2(math.e), w).reshape(SWA_KV_HEADS, g_per * w)
    kv_spec = lambda j: pl.BlockSpec((seq, LANES), lambda b, t: (b, kv_grp + j))
    return pl.pallas_call(
        functools.partial(_swa_kernel, seq=seq),
        grid=(batch, n_t),
        in_specs=[
            pl.BlockSpec((SWA_KV_HEADS, g_per * w), lambda b, t: (0, 0)),
            pl.BlockSpec((nq, SWA_SUB * w), lambda b, t: (0, b * n_t + t)),
            kv_spec(0), kv_spec(1), kv_spec(2),
        ],
        out_specs=pl.BlockSpec((SWA_SUB * w, nq), lambda b, t: (b * n_t + t, 0)),
        out_shape=jax.ShapeDtypeStruct((m, nq), BF16),
        scratch_shapes=[pltpu.VMEM((seq // w, 2 * LANES, w), BF16),
                        pltpu.VMEM((SWA_SUB * SWA_KV_HEADS, 2 * w, g_per * w), F32)],
        compiler_params=_cparams(2),
        name="swa",
    )(sink_rows, qt, ob, ob, ob)


MOBA_BIAS_ROWS = 16
MOBA_Q_SCALE = MOBA_HEAD_DIM ** -0.5 * math.log2(math.e)


def _moba_kernel(qt_ref, k_ref, vt, o_ref, kaug, kmean, qaug, s_a, s_b, *, seq, cb, qb):
    blk = MOBA_BLOCK
    dh = MOBA_HEAD_DIM
    nblk = seq // blk
    kc = cb * blk
    wq = qb * blk
    nbr = MOBA_BIAS_ROWS
    i = pl.program_id(2)

    @pl.when(i == 0)
    def _():
        k = k_ref[...]
        kaug[:, :dh] = k
        row_blk = lax.broadcasted_iota(jnp.int32, (seq, LANES), 0) // blk
        lane = lax.broadcasted_iota(jnp.int32, (seq, LANES), 1)
        kaug[:, dh:] = (row_blk == lane).astype(BF16)
        kmean[...] = jnp.zeros_like(kmean)
        kmean[:nblk, :] = jnp.mean(k.astype(F32).reshape(nblk, blk, dh), axis=1)
        qaug[dh + nbr:, :] = jnp.zeros((dh - nbr, wq), BF16)

    q_t = qt_ref[...]
    gate = jnp.dot(kmean[...].astype(BF16), q_t, preferred_element_type=F32)
    row = lax.broadcasted_iota(jnp.int32, (nbr, wq), 0)
    rowf = row.astype(F32)
    own = i * qb + lax.broadcasted_iota(jnp.int32, (nbr, wq), 1) // blk
    past = row < own
    g = jnp.where(past, gate, -jnp.inf)
    sel = row == own
    for _ in range(min(MOBA_TOPK, nblk - 1)):
        mx = jnp.max(g, axis=0, keepdims=True)
        idx = jnp.min(jnp.where(g == mx, rowf, float(nbr)), axis=0, keepdims=True)
        hit = rowf == idx
        sel = sel | (hit & past)
        g = jnp.where(hit, -jnp.inf, g)
    qaug[:dh, :] = q_t
    qaug[dh:dh + nbr, :] = jnp.where(sel, 0.0, NEG).astype(BF16)

    c_own = (i * qb) // cb
    qpos = i * wq + lax.broadcasted_iota(jnp.int32, (kc, wq), 1)
    for case in range(nblk // cb):
        @pl.when(c_own == case)
        def _(case=case):
            order = [case] + list(range(case))
            s_bufs = (s_a, s_b)

            def stage_scores(idx):
                c = order[idx]
                s = jnp.dot(kaug[c * kc:(c + 1) * kc, :], qaug[...],
                            preferred_element_type=F32)
                if c == case:
                    kpos = c * kc + lax.broadcasted_iota(jnp.int32, (kc, wq), 0)
                    s = jnp.where(kpos <= qpos, s, NEG)
                s_bufs[idx % 2][...] = s

            stage_scores(0)
            m_col = l_col = acc = None
            for idx, c in enumerate(order):
                if idx + 1 < len(order):
                    stage_scores(idx + 1)
                rows = slice(c * kc, (c + 1) * kc)
                s = s_bufs[idx % 2][...]
                m_c = jnp.max(s, axis=0, keepdims=True)
                if m_col is None:
                    m_col = m_c
                    p = jnp.exp2(s - m_col)
                    l_col = jnp.sum(p, axis=0, keepdims=True)
                    acc = jnp.dot(vt[:, rows], p.astype(BF16), preferred_element_type=F32)
                else:
                    m_new = jnp.maximum(m_col, m_c)
                    alpha = jnp.exp2(m_col - m_new)
                    p = jnp.exp2(s - m_new)
                    l_col = alpha * l_col + jnp.sum(p, axis=0, keepdims=True)
                    acc = alpha * acc + jnp.dot(vt[:, rows], p.astype(BF16),
                                                preferred_element_type=F32)
                    m_col = m_new
            o_ref[...] = (acc / l_col).T.astype(BF16)


def _moba(qvt, ob, batch, seq, cb, qb):
    m = batch * seq
    blk = MOBA_BLOCK
    dh = MOBA_HEAD_DIM
    nh = MOBA_HEADS
    n_q = seq // (qb * blk)
    wq = qb * blk
    return pl.pallas_call(
        functools.partial(_moba_kernel, seq=seq, cb=cb, qb=qb),
        grid=(batch, nh, n_q),
        in_specs=[
            pl.BlockSpec((dh, wq), lambda b, h, i: (h, b * n_q + i)),
            pl.BlockSpec((seq, dh), lambda b, h, i: (b, h)),
            pl.BlockSpec((dh, seq), lambda b, h, i: (nh + h, b)),
        ],
        out_specs=pl.BlockSpec((wq, dh), lambda b, h, i: (b * n_q + i, h)),
        out_shape=jax.ShapeDtypeStruct((m, nh * dh), BF16),
        scratch_shapes=[pltpu.VMEM((seq, 2 * dh), BF16),
                        pltpu.VMEM((MOBA_BIAS_ROWS, dh), F32),
                        pltpu.VMEM((2 * dh, wq), BF16),
                        pltpu.VMEM((cb * blk, wq), F32),
                        pltpu.VMEM((cb * blk, wq), F32)],
        compiler_params=_cparams(3),
        name="moba",
    )(qvt, ob, qvt)


def _ret_kernel(qk_ref, v_ref, dec_ref, xi_ref, zeta_ref, o_ref, r_scr, *, n_chunk, g_chunk):
    t = RET_CHUNK
    nh, dk, dv = RET_HEADS, RET_QK_DIM, RET_V_DIM

    @pl.when(pl.program_id(1) == 0)
    def _():
        r_scr[...] = jnp.zeros_like(r_scr)

    def body(c, carry):
        r0 = pl.multiple_of(c * t, t)
        for h in range(nh):
            q = qk_ref[pl.ds(r0, t), h * dk:(h + 1) * dk]
            k = qk_ref[pl.ds(r0, t), (nh + h) * dk:(nh + h + 1) * dk]
            v = v_ref[pl.ds(r0, t), h * dv:(h + 1) * dv]
            s = lax.dot_general(q, k, _NT, preferred_element_type=F32) * dec_ref[h]
            inner = jnp.dot(s.astype(BF16), v, preferred_element_type=F32)
            r_prev = r_scr[h]
            q_x = (q.astype(F32) * xi_ref[h]).astype(BF16)
            cross = jnp.dot(q_x, r_prev.astype(BF16), preferred_element_type=F32)
            o = inner + cross
            o = o * lax.rsqrt(jnp.mean(o * o, axis=-1, keepdims=True) + EPS)
            o_ref[pl.ds(r0, t), h * dv:(h + 1) * dv] = o.astype(BF16)
            k_z = (k.astype(F32) * zeta_ref[h]).astype(BF16)
            kv = lax.dot_general(k_z, v, _TN, preferred_element_type=F32)
            r_scr[h] = g_chunk[h] * r_prev + kv
        return carry

    lax.fori_loop(0, n_chunk, body, 0, unroll=True)


def _ret_constants():
    nh, t, dk = RET_HEADS, RET_CHUNK, RET_QK_DIM
    lin = [math.log(1.0 / 32) + (math.log(1.0 / 512) - math.log(1.0 / 32)) * h / (nh - 1)
           for h in range(nh)]
    log_g = [math.log1p(-math.exp(v)) for v in lin]
    i = jnp.arange(t, dtype=F32)
    lg = jnp.asarray(log_g, F32)
    diff = i[:, None] - i[None, :]
    decay = jnp.where(diff >= 0, jnp.exp(jnp.maximum(diff, 0.0)[None] * lg[:, None, None]), 0.0)
    xi = jnp.exp((i + 1)[None, :] * lg[:, None])
    zeta = jnp.exp((t - 1 - i)[None, :] * lg[:, None])
    xi_t = jnp.broadcast_to(xi[:, :, None], (nh, t, dk))
    zeta_t = jnp.broadcast_to(zeta[:, :, None], (nh, t, dk))
    g_chunk = tuple(math.exp(t * v) for v in log_g)
    return decay, xi_t, zeta_t, g_chunk


def _ret(ob, batch, seq, seg):
    m = batch * seq
    nh, dk, dv, t = RET_HEADS, RET_QK_DIM, RET_V_DIM, RET_CHUNK
    n_seg = seq // seg
    decay, xi_t, zeta_t, g_chunk = _ret_constants()
    wqk = 2 * nh * dk
    wv = nh * dv
    const = lambda b, s: (0, 0, 0)
    return pl.pallas_call(
        functools.partial(_ret_kernel, n_chunk=seg // t, g_chunk=g_chunk),
        grid=(batch, n_seg),
        in_specs=[
            pl.BlockSpec((seg, wqk), lambda b, s: (b * n_seg + s, 0)),
            pl.BlockSpec((seg, wv), lambda b, s: (b * n_seg + s, wqk // wv)),
            pl.BlockSpec((nh, t, t), const),
            pl.BlockSpec((nh, t, dk), const),
            pl.BlockSpec((nh, t, dk), const),
        ],
        out_specs=pl.BlockSpec((seg, wv), lambda b, s: (b * n_seg + s, 0)),
        out_shape=jax.ShapeDtypeStruct((m, wv), BF16),
        scratch_shapes=[pltpu.VMEM((nh, dk, dv), F32)],
        compiler_params=_cparams(2),
        name="retention",
    )(ob, ob, decay, xi_t, zeta_t)


def _out_kernel(mix_ref, qm_ref, z_ref, x_ref, mk_ref, mv_ref, w_hbm, fn_ref, o_ref, y_scr, w_ref,
                stage, sem, *, final):
    dm = MEM_HEAD_DIM
    scale = dm ** -0.5
    n_slices = BRANCH_WIDTH // COL_TILE
    rows = stage.shape[1]
    per = COL_TILE // rows
    start, fetch_piece = _weight_stream(
        w_hbm, w_ref, stage, sem,
        [slice(j * rows, (j + 1) * rows) for j in range(BRANCH_WIDTH // rows)], axis=0)

    def fetch(t):
        for j in range(per):
            fetch_piece(t * per + j)

    def body(first_step):
        if first_step:
            start()
        out = x_ref[...]
        for t in range(MIX_WIDTH // COL_TILE):
            lo, hi = t * COL_TILE, (t + 1) * COL_TILE
            y_t = (mix_ref[:, lo:hi].astype(F32) * _silu(z_ref[:, lo:hi])).astype(BF16)
            if first_step:
                fetch(t)
            out = out + jnp.dot(y_t, w_ref[lo:hi, :], preferred_element_type=F32)
        for h in range(MEM_HEADS):
            lo, hi = h * dm, (h + 1) * dm
            s = lax.dot_general(qm_ref[:, lo:hi], mk_ref[:, lo:hi], _NT,
                                preferred_element_type=F32) * scale
            p = jnp.exp(s - jnp.max(s, axis=-1, keepdims=True))
            l = jnp.sum(p, axis=-1, keepdims=True)
            o = jnp.dot(p.astype(BF16), mv_ref[:, lo:hi], preferred_element_type=F32) / l
            y_scr[:, lo:hi] = (o * _silu(z_ref[:, MIX_WIDTH + lo:MIX_WIDTH + hi])).astype(BF16)
        if first_step:
            fetch(n_slices - 1)
        out = out + jnp.dot(y_scr[...], w_ref[MIX_WIDTH:, :], preferred_element_type=F32)
        if final:
            ms = jnp.mean(out * out, axis=-1, keepdims=True)
            out = (out * lax.rsqrt(ms + EPS)) * fn_ref[...]
        o_ref[...] = out

    @pl.when(pl.program_id(0) == 0)
    def _():
        body(True)

    @pl.when(pl.program_id(0) != 0)
    def _():
        body(False)


def _out_proj(mix, ob, qm_blk, z, x2d, mkv, w_out, final_norm, batch, seq, tm, final):
    m, d = x2d.shape
    per_b = seq // tm
    row = lambda i: (i, 0)
    return pl.pallas_call(
        functools.partial(_out_kernel, final=final),
        grid=(m // tm,),
        in_specs=[
            pl.BlockSpec((tm, MIX_WIDTH), row),
            pl.BlockSpec((tm, MEM_WIDTH), lambda i: (i, qm_blk)),
            pl.BlockSpec((tm, BRANCH_WIDTH), row),
            pl.BlockSpec((tm, d), row),
            pl.BlockSpec((N_MEM, MEM_WIDTH), lambda i: (i // per_b, 0)),
            pl.BlockSpec((N_MEM, MEM_WIDTH), lambda i: (i // per_b, 1)),
            pl.BlockSpec(memory_space=pl.ANY),
            pl.BlockSpec((1, d), lambda i: (0, 0)),
        ],
        out_specs=pl.BlockSpec((tm, d), row),
        out_shape=jax.ShapeDtypeStruct((m, d), F32),
        scratch_shapes=[pltpu.VMEM((tm, MEM_WIDTH), BF16),
                        pltpu.VMEM((BRANCH_WIDTH, d), BF16),
                        pltpu.VMEM((2, COL_TILE // 2, d), F32),
                        pltpu.SemaphoreType.DMA((2,))],
        compiler_params=_cparams(1),
        name="out_proj",
    )(mix, ob, z, x2d, mkv, mkv, w_out, final_norm.reshape(1, d))


def _in_proj_plan(mixer):
    n_qm = MEM_WIDTH // LANES
    n_z = BRANCH_WIDTH // LANES
    if mixer == 0:
        n_q = SWA_Q_HEADS * SWA_HEAD_DIM // LANES
        plan = [("RST", gq) for gq in range(n_q)]
        plan += [("R", n_qm), ("Rh", n_qm + 1), ("P", n_qm + 2)]
        plan += [("P", j) for j in range(n_qm)]
    elif mixer == 1:
        n_h = MIX_WIDTH // LANES
        plan = [("RST", j) for j in range(n_h)] + [("R", j) for j in range(n_h)]
        plan += [("PT", n_h + j) for j in range(n_h)] + [("P", n_h + j) for j in range(n_qm)]
    else:
        n_qk = RET_HEADS * RET_QK_DIM // LANES
        kinds = ["R"] * n_qk + ["RS"] * n_qk + ["P"] * (MIX_WIDTH // LANES + n_qm)
        plan = [(kind, j) for j, kind in enumerate(kinds)]
    return plan + [("P", None)] * n_z


def kernel(x, mem, positions, mem_norm, w_mem_kv, norm_0, w_in_0, sinks_0, w_out_0, norm_1, w_in_1,
           w_out_1, norm_2, w_in_2, w_out_2, norm_3, w_in_3, sinks_3, w_out_3, final_norm):
    batch, seq, d = x.shape
    m = batch * seq
    layers = [(norm_0, w_in_0, w_out_0, sinks_0), (norm_1, w_in_1, w_out_1, None),
              (norm_2, w_in_2, w_out_2, None), (norm_3, w_in_3, w_out_3, sinks_3)]

    mkv = _mem_kv(mem.reshape(batch * N_MEM, d), mem_norm, w_mem_kv.astype(BF16))

    pos_col = positions.astype(F32).reshape(m, 1)
    rope = (
        (_rope_lane_rows(SWA_HEAD_DIM, SWA_HEAD_DIM // ROPE_FRACTION, ROPE_THETA),
         SWA_HEAD_DIM // ROPE_FRACTION // 2),
        (_rope_lane_rows(MOBA_HEAD_DIM, MOBA_HEAD_DIM // ROPE_FRACTION, ROPE_THETA),
         MOBA_HEAD_DIM // ROPE_FRACTION // 2),
        (_rope_lane_rows(RET_QK_DIM, RET_QK_DIM, RET_THETA), RET_QK_DIM // 2),
    )

    tm_in = min(256, m)
    tm_out = min(512, seq)
    h = x.reshape(m, d)
    n_layers = len(layers)
    for li, (g, w_in, w_out, sinks) in enumerate(layers):
        mixer = li % N_MIXERS
        lane_rows, half = rope[mixer]
        rs_scale = (SWA_Q_SCALE, MOBA_Q_SCALE, RET_QK_DIM ** -0.5)[mixer]
        ob, z, *qt = _in_proj(h, g, w_in, pos_col, lane_rows, _in_proj_plan(mixer), half,
                              rs_scale, tm_in)
        if mixer == 0:
            mix = _swa(qt[0], ob, sinks, batch, seq)
            qm_blk = 0
        elif mixer == 1:
            mix = _moba(qt[0], ob, batch, seq, cb=min(4, seq // MOBA_BLOCK), qb=4)
            qm_blk = MIX_WIDTH // MEM_WIDTH
        else:
            mix = _ret(ob, batch, seq, seg=min(1024, seq))
            qm_blk = (2 * RET_HEADS * RET_QK_DIM + MIX_WIDTH) // MEM_WIDTH
        h = _out_proj(mix, ob, qm_blk, z, h, mkv, w_out, final_norm, batch, seq,
                      tm_out, final=(li == n_layers - 1))
    return h.reshape(batch, seq, d)
```

```python
import functools
import math

import jax
import jax.numpy as jnp
from jax import lax
from jax.experimental import pallas as pl
from jax.experimental.pallas import tpu as pltpu

F32 = jnp.float32
BF16 = jnp.bfloat16

D_MODEL = 2048
N_MEM = 256
N_MIXERS = 3
BRANCH_WIDTH = D_MODEL
MEM_HEADS = 4
MEM_HEAD_DIM = 128
MEM_WIDTH = MEM_HEADS * MEM_HEAD_DIM
MIX_WIDTH = BRANCH_WIDTH - MEM_WIDTH

SWA_HEAD_DIM = 64
SWA_Q_HEADS = MIX_WIDTH // SWA_HEAD_DIM
SWA_KV_HEADS = SWA_Q_HEADS // 8
SWA_WINDOW = 128

MOBA_HEAD_DIM = 128
MOBA_HEADS = MIX_WIDTH // MOBA_HEAD_DIM
MOBA_BLOCK = 256
MOBA_TOPK = 3

RET_HEADS = 6
RET_V_DIM = MIX_WIDTH // RET_HEADS
RET_QK_DIM = RET_V_DIM // 2
RET_CHUNK = 128
RET_THETA = 10000.0

ROPE_THETA = 500000.0
ROPE_FRACTION = 4
EPS = 1e-6

LANES = 128
COL_TILE = 512
NEG = -1e30
VMEM_LIMIT = 56 * 1024 * 1024

_NT = (((1,), (1,)), ((), ()))
_TN = (((0,), (0,)), ((), ()))


def _cparams(n_axes):
    return pltpu.CompilerParams(dimension_semantics=("arbitrary",) * n_axes,
                                vmem_limit_bytes=VMEM_LIMIT)


def _tile_sizes(m, seq):
    return {
        "in_rows": (min(512, m), min(256, m), min(512, m)),
        "out_rows": min(512, seq),
        "moba_blocks": min(4, seq // MOBA_BLOCK),
        "ret_rows": min(1024, seq),
    }


def _silu(z):
    return z * (1.0 / (1.0 + jnp.exp(-z)))


def _weight_stream(w_hbm, w_scr, stage, sem, slices, axis):
    def window(ref, sl, lead=()):
        idx = (sl, slice(None)) if axis == 0 else (slice(None), sl)
        return ref.at[lead + idx]

    def copy(i):
        sl = slices[i]
        local = slice(0, sl.stop - sl.start)
        return pltpu.make_async_copy(window(w_hbm, sl), window(stage, local, (i % 2,)),
                                     sem.at[i % 2])

    def start():
        copy(0).start()

    def fetch(i):
        if i + 1 < len(slices):
            copy(i + 1).start()
        copy(i).wait()
        sl = slices[i]
        local = slice(0, sl.stop - sl.start)
        idx = (sl, slice(None)) if axis == 0 else (slice(None), sl)
        loc = (local, slice(None)) if axis == 0 else (slice(None), local)
        w_scr[idx] = stage[(i % 2,) + loc].astype(BF16)

    return start, fetch


def _in_proj_kernel(x_ref, g_ref, w_hbm, pos_ref, lane_ref, ob_ref, oz_ref, *rest, plan, half,
                    rs_scale):
    qt_ref = rest[0] if len(rest) == 5 else None
    h_scr, w_ref, stage, sem = rest[-4:]
    gpt = COL_TILE // LANES
    n_grp = len(plan)
    tiles = [(g0, min(g0 + gpt, n_grp)) for g0 in range(0, n_grp, gpt)]
    gpp = stage.shape[2] // LANES
    pieces = [(p0, min(p0 + gpp, g1)) for g0, g1 in tiles for p0 in range(g0, g1, gpp)]
    start, fetch_piece = _weight_stream(
        w_hbm, w_ref, stage, sem, [slice(p0 * LANES, p1 * LANES) for p0, p1 in pieces], axis=1)

    def fetch(ti):
        for pi, (p0, _) in enumerate(pieces):
            if tiles[ti][0] <= p0 < tiles[ti][1]:
                fetch_piece(pi)

    def body(first_step):
        if first_step:
            start()
        x = x_ref[...]
        ms = jnp.mean(x * x, axis=-1, keepdims=True)
        h_scr[...] = ((x * lax.rsqrt(ms + EPS)) * g_ref[...]).astype(BF16)

        tables = {}

        def rope(a, pat):
            if 0 not in tables:
                ang = pos_ref[...] * lane_ref[0:1, :]
                sn = jnp.sin(ang)
                tables[0] = (jnp.cos(ang), sn * lane_ref[1:2, :], sn * lane_ref[2:3, :])
            if pat not in tables:
                first = lax.broadcasted_iota(jnp.int32, tables[0][0].shape, 1) < LANES // 2
                c0, sp0, sm0 = tables[0]
                tables[pat] = (jnp.where(first, c0, 1.0), jnp.where(first, sp0, 0.0),
                               jnp.where(first, sm0, 0.0))
            c, s_plus, s_minus = tables[pat]
            out = a * c + pltpu.roll(a, half, 1) * s_plus
            if 2 * half != LANES:
                out = out + pltpu.roll(a, LANES - half, 1) * s_minus
            return out

        for ti, (g0, g1) in enumerate(tiles):
            if first_step:
                fetch(ti)
            acc = jnp.dot(h_scr[...], w_ref[:, g0 * LANES:g1 * LANES],
                          preferred_element_type=F32)
            for gi in range(g0, g1):
                kind, dest = plan[gi]
                a = acc[:, (gi - g0) * LANES:(gi - g0 + 1) * LANES]
                if dest is None:
                    zc = (gi - (n_grp - oz_ref.shape[1] // LANES)) * LANES
                    oz_ref[:, zc:zc + LANES] = a
                    continue
                if kind in ("R", "RS", "RST", "Rh"):
                    a = rope(a, 1 if kind == "Rh" else 0)
                if kind in ("RS", "RST"):
                    a = a * rs_scale
                if kind in ("RST", "PT"):
                    qt_ref[dest * LANES:(dest + 1) * LANES, :] = a.T.astype(BF16)
                else:
                    ob_ref[:, dest * LANES:(dest + 1) * LANES] = a.astype(BF16)

    @pl.when(pl.program_id(0) == 0)
    def _():
        body(True)

    @pl.when(pl.program_id(0) != 0)
    def _():
        body(False)


def _rope_lane_rows(head_dim, rot_dim, theta):
    half = rot_dim // 2
    inv = theta ** (-jnp.arange(0, rot_dim, 2, dtype=F32) / rot_dim)
    rest = head_dim - rot_dim
    z_half, z_rest = jnp.zeros((half,), F32), jnp.zeros((rest,), F32)
    ones = jnp.ones((half,), F32)
    inv_h = jnp.concatenate([inv, inv, z_rest])
    plus_h = jnp.concatenate([z_half, ones, z_rest])
    minus_h = jnp.concatenate([-ones, z_half, z_rest])
    if 2 * half == LANES:
        plus_h, minus_h = plus_h + minus_h, jnp.zeros_like(minus_h)
    rep = LANES // head_dim
    return jnp.stack([jnp.tile(r, rep) for r in (inv_h, plus_h, minus_h)])


def _in_proj(x2d, g, w_in, pos_col, lane_rows, plan, half, rs_scale, tm):
    m, d = x2d.shape
    n_tot = w_in.shape[1]
    n_bf = (max(dest for kind, dest in plan if dest is not None and kind[-1] != "T") + 1) * LANES
    n_qt = sum(kind[-1] == "T" for kind, _ in plan) * LANES
    row = lambda i: (i, 0)
    fixed = lambda i: (0, 0)
    out_specs = [pl.BlockSpec((tm, n_bf), row), pl.BlockSpec((tm, BRANCH_WIDTH), row)]
    out_shape = [jax.ShapeDtypeStruct((m, n_bf), BF16), jax.ShapeDtypeStruct((m, BRANCH_WIDTH), F32)]
    if n_qt:
        out_specs.append(pl.BlockSpec((n_qt, tm), lambda i: (0, i)))
        out_shape.append(jax.ShapeDtypeStruct((n_qt, m), BF16))
    return pl.pallas_call(
        functools.partial(_in_proj_kernel, plan=tuple(plan), half=half, rs_scale=rs_scale),
        grid=(m // tm,),
        in_specs=[
            pl.BlockSpec((tm, d), row),
            pl.BlockSpec((1, d), fixed),
            pl.BlockSpec(memory_space=pl.ANY),
            pl.BlockSpec((tm, 1), row),
            pl.BlockSpec(lane_rows.shape, fixed),
        ],
        out_specs=out_specs,
        out_shape=out_shape,
        scratch_shapes=[pltpu.VMEM((tm, d), BF16),
                        pltpu.VMEM((d, n_tot), BF16),
                        pltpu.VMEM((2, d, COL_TILE // 2), F32),
                        pltpu.SemaphoreType.DMA((2,))],
        compiler_params=_cparams(1),
        name="in_proj",
    )(x2d, g.reshape(1, d), w_in, pos_col, lane_rows)


def _mem_kv_kernel(x_ref, g_ref, w_ref, o_ref):
    x = x_ref[...]
    ms = jnp.mean(x * x, axis=-1, keepdims=True)
    h = ((x * lax.rsqrt(ms + EPS)) * g_ref[...]).astype(BF16)
    o_ref[...] = jnp.dot(h, w_ref[...], preferred_element_type=F32).astype(BF16)


def _mem_kv(mem2d, g, w_bf16):
    m, d = mem2d.shape
    n = w_bf16.shape[1]
    tm = min(m, 256)
    return pl.pallas_call(
        _mem_kv_kernel,
        grid=(m // tm,),
        in_specs=[
            pl.BlockSpec((tm, d), lambda i: (i, 0)),
            pl.BlockSpec((1, d), lambda i: (0, 0)),
            pl.BlockSpec((d, n), lambda i: (0, 0)),
        ],
        out_specs=pl.BlockSpec((tm, n), lambda i: (i, 0)),
        out_shape=jax.ShapeDtypeStruct((m, n), BF16),
        compiler_params=_cparams(1),
        name="mem_kv",
    )(mem2d, g.reshape(1, d), w_bf16)


SWA_Q_SCALE = SWA_HEAD_DIM ** -0.5 * math.log2(math.e)


SWA_SUB = 4


def _swa_kernel(sink_ref, qt_ref, ka_ref, kb_ref, kc_ref, o_ref, vt, s_scr, *, seq):
    w = SWA_WINDOW
    dh = SWA_HEAD_DIM
    g_per = SWA_Q_HEADS // SWA_KV_HEADS
    step = pl.program_id(1)

    @pl.when(step == 0)
    def _():
        for j in range(seq // w):
            rows = slice(j * w, (j + 1) * w)
            both = jnp.concatenate([kb_ref[rows, :], kc_ref[rows, :]], axis=1)
            vt[j] = both.astype(F32).T.astype(BF16)

    zeros = jnp.zeros((dh, g_per * w), BF16)
    vt_wins = []
    for sub in range(SWA_SUB):
        t = step * SWA_SUB + sub
        j0 = jnp.maximum(t - 1, 0)
        r0 = pl.multiple_of(j0 * w, w)
        k_wins = (ka_ref[pl.ds(r0, 2 * w), :], kb_ref[pl.ds(r0, 2 * w), :])
        vt_wins.append(jnp.concatenate([vt[j0], vt[j0 + 1]], axis=1))
        kpos = r0 + lax.broadcasted_iota(jnp.int32, (2 * w, w), 0)
        qpos = t * w + lax.broadcasted_iota(jnp.int32, (2 * w, w), 1)
        bias = jnp.where((kpos <= qpos) & (kpos > qpos - w), 0.0, NEG)
        bias = jnp.concatenate([bias] * g_per, axis=1)
        for h in range(SWA_KV_HEADS):
            q_grp = jnp.concatenate(
                [qt_ref[(h * g_per + g) * dh:(h * g_per + g + 1) * dh, sub * w:(sub + 1) * w]
                 for g in range(g_per)], axis=1)
            q_pad = jnp.concatenate([q_grp, zeros] if h % 2 == 0 else [zeros, q_grp], axis=0)
            s_scr[sub * SWA_KV_HEADS + h] = (
                jnp.dot(k_wins[h // 2], q_pad, preferred_element_type=F32) + bias)
    for sub in range(SWA_SUB):
        for h in range(SWA_KV_HEADS):
            s = s_scr[sub * SWA_KV_HEADS + h]
            sink = sink_ref[h:h + 1, :]
            mx = jnp.maximum(jnp.max(s, axis=0, keepdims=True), sink)
            p = jnp.exp2(s - mx)
            denom = jnp.sum(p, axis=0, keepdims=True) + jnp.exp2(sink - mx)
            o_t = jnp.dot(vt_wins[sub][(h + 1) * dh:(h + 2) * dh, :], p.astype(BF16),
                          preferred_element_type=F32) / denom
            for pair in range(g_per // 2):
                two = jnp.concatenate([o_t[:, (2 * pair) * w:(2 * pair + 1) * w],
                                       o_t[:, (2 * pair + 1) * w:(2 * pair + 2) * w]], axis=0)
                c0 = (h * g_per + 2 * pair) * dh
                o_ref[sub * w:(sub + 1) * w, c0:c0 + 2 * dh] = two.T.astype(BF16)


def _swa(qt, ob, sinks, batch, seq):
    m = batch * seq
    w = SWA_WINDOW
    nq = SWA_Q_HEADS * SWA_HEAD_DIM
    g_per = SWA_Q_HEADS // SWA_KV_HEADS
    n_t = seq // (SWA_SUB * w)
    kv_grp = MEM_WIDTH // LANES
    sink_rows = jnp.repeat(sinks.astype(F32) * math.log2(math.e), w).reshape(SWA_KV_HEADS, g_per * w)
    kv_spec = lambda j: pl.BlockSpec((seq, LANES), lambda b, t: (b, kv_grp + j))
    return pl.pallas_call(
        functools.partial(_swa_kernel, seq=seq),
        grid=(batch, n_t),
        in_specs=[
            pl.BlockSpec((SWA_KV_HEADS, g_per * w), lambda b, t: (0, 0)),
            pl.BlockSpec((nq, SWA_SUB * w), lambda b, t: (0, b * n_t + t)),
            kv_spec(0), kv_spec(1), kv_spec(2),
        ],
        out_specs=pl.BlockSpec((SWA_SUB * w, nq), lambda b, t: (b * n_t + t, 0)),
        out_shape=jax.ShapeDtypeStruct((m, nq), BF16),
        scratch_shapes=[pltpu.VMEM((seq // w, 2 * LANES, w), BF16),
                        pltpu.VMEM((SWA_SUB * SWA_KV_HEADS, 2 * w, g_per * w), F32)],
        compiler_params=_cparams(2),
        name="swa",
    )(sink_rows, qt, ob, ob, ob)


MOBA_BIAS_ROWS = 16
MOBA_Q_SCALE = MOBA_HEAD_DIM ** -0.5 * math.log2(math.e)


def _moba_kernel(qt_ref, k_ref, vt, o_ref, kaug, kmean, qaug, s_a, s_b, *, seq, cb, qb):
    blk = MOBA_BLOCK
    dh = MOBA_HEAD_DIM
    nblk = seq // blk
    kc = cb * blk
    wq = qb * blk
    nbr = MOBA_BIAS_ROWS
    i = pl.program_id(2)

    @pl.when(i == 0)
    def _():
        k = k_ref[...]
        kaug[:, :dh] = k
        row_blk = lax.broadcasted_iota(jnp.int32, (seq, LANES), 0) // blk
        lane = lax.broadcasted_iota(jnp.int32, (seq, LANES), 1)
        kaug[:, dh:] = (row_blk == lane).astype(BF16)
        kmean[...] = jnp.zeros_like(kmean)
        kmean[:nblk, :] = jnp.mean(k.astype(F32).reshape(nblk, blk, dh), axis=1)
        qaug[dh + nbr:, :] = jnp.zeros((dh - nbr, wq), BF16)

    q_t = qt_ref[...]
    gate = jnp.dot(kmean[...].astype(BF16), q_t, preferred_element_type=F32)
    row = lax.broadcasted_iota(jnp.int32, (nbr, wq), 0)
    rowf = row.astype(F32)
    own = i * qb + lax.broadcasted_iota(jnp.int32, (nbr, wq), 1) // blk
    past = row < own
    g = jnp.where(past, gate, -jnp.inf)
    sel = row == own
    for _ in range(min(MOBA_TOPK, nblk - 1)):
        mx = jnp.max(g, axis=0, keepdims=True)
        idx = jnp.min(jnp.where(g == mx, rowf, float(nbr)), axis=0, keepdims=True)
        hit = rowf == idx
        sel = sel | (hit & past)
        g = jnp.where(hit, -jnp.inf, g)
    qaug[:dh, :] = q_t
    qaug[dh:dh + nbr, :] = jnp.where(sel, 0.0, NEG).astype(BF16)

    c_own = (i * qb) // cb
    qpos = i * wq + lax.broadcasted_iota(jnp.int32, (kc, wq), 1)
    for case in range(nblk // cb):
        @pl.when(c_own == case)
        def _(case=case):
            order = [case] + list(range(case))
            s_bufs = (s_a, s_b)

            def stage_scores(idx):
                c = order[idx]
                s = jnp.dot(kaug[c * kc:(c + 1) * kc, :], qaug[...],
                            preferred_element_type=F32)
                if c == case:
                    kpos = c * kc + lax.broadcasted_iota(jnp.int32, (kc, wq), 0)
                    s = jnp.where(kpos <= qpos, s, NEG)
                s_bufs[idx % 2][...] = s

            stage_scores(0)
            m_col = l_col = acc = None
            for idx, c in enumerate(order):
                if idx + 1 < len(order):
                    stage_scores(idx + 1)
                rows = slice(c * kc, (c + 1) * kc)
                s = s_bufs[idx % 2][...]
                m_c = jnp.max(s, axis=0, keepdims=True)
                if m_col is None:
                    m_col = m_c
                    p = jnp.exp2(s - m_col)
                    l_col = jnp.sum(p, axis=0, keepdims=True)
                    acc = jnp.dot(vt[:, rows], p.astype(BF16), preferred_element_type=F32)
                else:
                    m_new = jnp.maximum(m_col, m_c)
                    alpha = jnp.exp2(m_col - m_new)
                    p = jnp.exp2(s - m_new)
                    l_col = alpha * l_col + jnp.sum(p, axis=0, keepdims=True)
                    acc = alpha * acc + jnp.dot(vt[:, rows], p.astype(BF16),
                                                preferred_element_type=F32)
                    m_col = m_new
            o_ref[...] = (acc / l_col).T.astype(BF16)


def _moba(qvt, ob, batch, seq, cb, qb):
    m = batch * seq
    blk = MOBA_BLOCK
    dh = MOBA_HEAD_DIM
    nh = MOBA_HEADS
    n_q = seq // (qb * blk)
    wq = qb * blk
    return pl.pallas_call(
        functools.partial(_moba_kernel, seq=seq, cb=cb, qb=qb),
        grid=(batch, nh, n_q),
        in_specs=[
            pl.BlockSpec((dh, wq), lambda b, h, i: (h, b * n_q + i)),
            pl.BlockSpec((seq, dh), lambda b, h, i: (b, h)),
            pl.BlockSpec((dh, seq), lambda b, h, i: (nh + h, b)),
        ],
        out_specs=pl.BlockSpec((wq, dh), lambda b, h, i: (b * n_q + i, h)),
        out_shape=jax.ShapeDtypeStruct((m, nh * dh), BF16),
        scratch_shapes=[pltpu.VMEM((seq, 2 * dh), BF16),
                        pltpu.VMEM((MOBA_BIAS_ROWS, dh), F32),
                        pltpu.VMEM((2 * dh, wq), BF16),
                        pltpu.VMEM((cb * blk, wq), F32),
                        pltpu.VMEM((cb * blk, wq), F32)],
        compiler_params=_cparams(3),
        name="moba",
    )(qvt, ob, qvt)


def _ret_kernel(qk_ref, v_ref, dec_ref, xi_ref, zeta_ref, o_ref, r_scr, *, n_chunk, g_chunk):
    t = RET_CHUNK
    nh, dk, dv = RET_HEADS, RET_QK_DIM, RET_V_DIM

    @pl.when(pl.program_id(1) == 0)
    def _():
        r_scr[...] = jnp.zeros_like(r_scr)

    def body(c, carry):
        r0 = pl.multiple_of(c * t, t)
        for h in range(nh):
            q = qk_ref[pl.ds(r0, t), h * dk:(h + 1) * dk]
            k = qk_ref[pl.ds(r0, t), (nh + h) * dk:(nh + h + 1) * dk]
            v = v_ref[pl.ds(r0, t), h * dv:(h + 1) * dv]
            s = lax.dot_general(q, k, _NT, preferred_element_type=F32) * dec_ref[h]
            inner = jnp.dot(s.astype(BF16), v, preferred_element_type=F32)
            r_prev = r_scr[h]
            q_x = (q.astype(F32) * xi_ref[h]).astype(BF16)
            cross = jnp.dot(q_x, r_prev.astype(BF16), preferred_element_type=F32)
            o = inner + cross
            o = o * lax.rsqrt(jnp.mean(o * o, axis=-1, keepdims=True) + EPS)
            o_ref[pl.ds(r0, t), h * dv:(h + 1) * dv] = o.astype(BF16)
            k_z = (k.astype(F32) * zeta_ref[h]).astype(BF16)
            kv = lax.dot_general(k_z, v, _TN, preferred_element_type=F32)
            r_scr[h] = g_chunk[h] * r_prev + kv
        return carry

    lax.fori_loop(0, n_chunk, body, 0, unroll=True)


def _ret_constants():
    nh, t, dk = RET_HEADS, RET_CHUNK, RET_QK_DIM
    lin = [math.log(1.0 / 32) + (math.log(1.0 / 512) - math.log(1.0 / 32)) * h / (nh - 1)
           for h in range(nh)]
    log_g = [math.log1p(-math.exp(v)) for v in lin]
    i = jnp.arange(t, dtype=F32)
    lg = jnp.asarray(log_g, F32)
    diff = i[:, None] - i[None, :]
    decay = jnp.where(diff >= 0, jnp.exp(jnp.maximum(diff, 0.0)[None] * lg[:, None, None]), 0.0)
    xi = jnp.exp((i + 1)[None, :] * lg[:, None])
    zeta = jnp.exp((t - 1 - i)[None, :] * lg[:, None])
    xi_t = jnp.broadcast_to(xi[:, :, None], (nh, t, dk))
    zeta_t = jnp.broadcast_to(zeta[:, :, None], (nh, t, dk))
    g_chunk = tuple(math.exp(t * v) for v in log_g)
    return decay, xi_t, zeta_t, g_chunk


def _ret(ob, batch, seq, seg):
    m = batch * seq
    nh, dk, dv, t = RET_HEADS, RET_QK_DIM, RET_V_DIM, RET_CHUNK
    n_seg = seq // seg
    decay, xi_t, zeta_t, g_chunk = _ret_constants()
    wqk = 2 * nh * dk
    wv = nh * dv
    const = lambda b, s: (0, 0, 0)
    return pl.pallas_call(
        functools.partial(_ret_kernel, n_chunk=seg // t, g_chunk=g_chunk),
        grid=(batch, n_seg),
        in_specs=[
            pl.BlockSpec((seg, wqk), lambda b, s: (b * n_seg + s, 0)),
            pl.BlockSpec((seg, wv), lambda b, s: (b * n_seg + s, wqk // wv)),
            pl.BlockSpec((nh, t, t), const),
            pl.BlockSpec((nh, t, dk), const),
            pl.BlockSpec((nh, t, dk), const),
        ],
        out_specs=pl.BlockSpec((seg, wv), lambda b, s: (b * n_seg + s, 0)),
        out_shape=jax.ShapeDtypeStruct((m, wv), BF16),
        scratch_shapes=[pltpu.VMEM((nh, dk, dv), F32)],
        compiler_params=_cparams(2),
        name="retention",
    )(ob, ob, decay, xi_t, zeta_t)


def _out_kernel(mix_ref, qm_ref, z_ref, x_ref, mk_ref, mv_ref, w_hbm, fn_ref, o_ref, y_scr, w_ref,
                stage, sem, *, final):
    dm = MEM_HEAD_DIM
    scale = dm ** -0.5
    n_slices = BRANCH_WIDTH // COL_TILE
    rows = stage.shape[1]
    per = COL_TILE // rows
    start, fetch_piece = _weight_stream(
        w_hbm, w_ref, stage, sem,
        [slice(j * rows, (j + 1) * rows) for j in range(BRANCH_WIDTH // rows)], axis=0)

    def fetch(t):
        for j in range(per):
            fetch_piece(t * per + j)

    def body(first_step):
        if first_step:
            start()
        out = x_ref[...]
        for t in range(MIX_WIDTH // COL_TILE):
            lo, hi = t * COL_TILE, (t + 1) * COL_TILE
            y_t = (mix_ref[:, lo:hi].astype(F32) * _silu(z_ref[:, lo:hi])).astype(BF16)
            if first_step:
                fetch(t)
            out = out + jnp.dot(y_t, w_ref[lo:hi, :], preferred_element_type=F32)
        for h in range(MEM_HEADS):
            lo, hi = h * dm, (h + 1) * dm
            s = lax.dot_general(qm_ref[:, lo:hi], mk_ref[:, lo:hi], _NT,
                                preferred_element_type=F32) * scale
            p = jnp.exp(s - jnp.max(s, axis=-1, keepdims=True))
            l = jnp.sum(p, axis=-1, keepdims=True)
            o = jnp.dot(p.astype(BF16), mv_ref[:, lo:hi], preferred_element_type=F32) / l
            y_scr[:, lo:hi] = (o * _silu(z_ref[:, MIX_WIDTH + lo:MIX_WIDTH + hi])).astype(BF16)
        if first_step:
            fetch(n_slices - 1)
        out = out + jnp.dot(y_scr[...], w_ref[MIX_WIDTH:, :], preferred_element_type=F32)
        if final:
            ms = jnp.mean(out * out, axis=-1, keepdims=True)
            out = (out * lax.rsqrt(ms + EPS)) * fn_ref[...]
        o_ref[...] = out

    @pl.when(pl.program_id(0) == 0)
    def _():
        body(True)

    @pl.when(pl.program_id(0) != 0)
    def _():
        body(False)


def _out_proj(mix, ob, qm_blk, z, x2d, mkv, w_out, final_norm, batch, seq, tm, final):
    m, d = x2d.shape
    per_b = seq // tm
    row = lambda i: (i, 0)
    return pl.pallas_call(
        functools.partial(_out_kernel, final=final),
        grid=(m // tm,),
        in_specs=[
            pl.BlockSpec((tm, MIX_WIDTH), row),
            pl.BlockSpec((tm, MEM_WIDTH), lambda i: (i, qm_blk)),
            pl.BlockSpec((tm, BRANCH_WIDTH), row),
            pl.BlockSpec((tm, d), row),
            pl.BlockSpec((N_MEM, MEM_WIDTH), lambda i: (i // per_b, 0)),
            pl.BlockSpec((N_MEM, MEM_WIDTH), lambda i: (i // per_b, 1)),
            pl.BlockSpec(memory_space=pl.ANY),
            pl.BlockSpec((1, d), lambda i: (0, 0)),
        ],
        out_specs=pl.BlockSpec((tm, d), row),
        out_shape=jax.ShapeDtypeStruct((m, d), F32),
        scratch_shapes=[pltpu.VMEM((tm, MEM_WIDTH), BF16),
                        pltpu.VMEM((BRANCH_WIDTH, d), BF16),
                        pltpu.VMEM((2, COL_TILE // 2, d), F32),
                        pltpu.SemaphoreType.DMA((2,))],
        compiler_params=_cparams(1),
        name="out_proj",
    )(mix, ob, z, x2d, mkv, mkv, w_out, final_norm.reshape(1, d))


def _in_proj_plan(mixer):
    n_qm = MEM_WIDTH // LANES
    n_z = BRANCH_WIDTH // LANES
    if mixer == 0:
        n_q = SWA_Q_HEADS * SWA_HEAD_DIM // LANES
        plan = [("RST", gq) for gq in range(n_q)]
        plan += [("R", n_qm), ("Rh", n_qm + 1), ("P", n_qm + 2)]
        plan += [("P", j) for j in range(n_qm)]
    elif mixer == 1:
        n_h = MIX_WIDTH // LANES
        plan = [("RST", j) for j in range(n_h)] + [("R", j) for j in range(n_h)]
        plan += [("PT", n_h + j) for j in range(n_h)] + [("P", n_h + j) for j in range(n_qm)]
    else:
        n_qk = RET_HEADS * RET_QK_DIM // LANES
        kinds = ["R"] * n_qk + ["RS"] * n_qk + ["P"] * (MIX_WIDTH // LANES + n_qm)
        plan = [(kind, j) for j, kind in enumerate(kinds)]
    return plan + [("P", None)] * n_z


def kernel(x, mem, positions, mem_norm, w_mem_kv, norm_0, w_in_0, sinks_0, w_out_0, norm_1, w_in_1,
           w_out_1, norm_2, w_in_2, w_out_2, norm_3, w_in_3, sinks_3, w_out_3, final_norm):
    batch, seq, d = x.shape
    m = batch * seq
    layers = [(norm_0, w_in_0, w_out_0, sinks_0), (norm_1, w_in_1, w_out_1, None),
              (norm_2, w_in_2, w_out_2, None), (norm_3, w_in_3, w_out_3, sinks_3)]

    mkv = _mem_kv(mem.reshape(batch * N_MEM, d), mem_norm, w_mem_kv.astype(BF16))

    pos_col = positions.astype(F32).reshape(m, 1)
    rope = (
        (_rope_lane_rows(SWA_HEAD_DIM, SWA_HEAD_DIM // ROPE_FRACTION, ROPE_THETA),
         SWA_HEAD_DIM // ROPE_FRACTION // 2),
        (_rope_lane_rows(MOBA_HEAD_DIM, MOBA_HEAD_DIM // ROPE_FRACTION, ROPE_THETA),
         MOBA_HEAD_DIM // ROPE_FRACTION // 2),
        (_rope_lane_rows(RET_QK_DIM, RET_QK_DIM, RET_THETA), RET_QK_DIM // 2),
    )

    tiles = _tile_sizes(m, seq)
    h = x.reshape(m, d)
    n_layers = len(layers)
    for li, (g, w_in, w_out, sinks) in enumerate(layers):
        mixer = li % N_MIXERS
        lane_rows, half = rope[mixer]
        rs_scale = (SWA_Q_SCALE, MOBA_Q_SCALE, RET_QK_DIM ** -0.5)[mixer]
        ob, z, *qt = _in_proj(h, g, w_in, pos_col, lane_rows, _in_proj_plan(mixer), half,
                              rs_scale, tiles["in_rows"][mixer])
        if mixer == 0:
            mix = _swa(qt[0], ob, sinks, batch, seq)
            qm_blk = 0
        elif mixer == 1:
            mix = _moba(qt[0], ob, batch, seq, cb=tiles["moba_blocks"], qb=tiles["moba_blocks"])
            qm_blk = MIX_WIDTH // MEM_WIDTH
        else:
            mix = _ret(ob, batch, seq, seg=tiles["ret_rows"])
            qm_blk = (2 * RET_HEADS * RET_QK_DIM + MIX_WIDTH) // MEM_WIDTH
        h = _out_proj(mix, ob, qm_blk, z, h, mkv, w_out, final_norm, batch, seq,
                      tiles["out_rows"], final=(li == n_layers - 1))
    return h.reshape(batch, seq, d)
```

```python
import functools
import math

import jax
import jax.numpy as jnp
from jax import lax
from jax.experimental import pallas as pl
from jax.experimental.pallas import tpu as pltpu

F32 = jnp.float32
BF16 = jnp.bfloat16

D_MODEL = 2048
N_MEM = 256
N_MIXERS = 3
BRANCH_WIDTH = D_MODEL
MEM_HEADS = 4
MEM_HEAD_DIM = 128
MEM_WIDTH = MEM_HEADS * MEM_HEAD_DIM
MIX_WIDTH = BRANCH_WIDTH - MEM_WIDTH

SWA_HEAD_DIM = 64
SWA_Q_HEADS = MIX_WIDTH // SWA_HEAD_DIM
SWA_KV_HEADS = SWA_Q_HEADS // 8
SWA_WINDOW = 128

MOBA_HEAD_DIM = 128
MOBA_HEADS = MIX_WIDTH // MOBA_HEAD_DIM
MOBA_BLOCK = 256
MOBA_TOPK = 3

RET_HEADS = 6
RET_V_DIM = MIX_WIDTH // RET_HEADS
RET_QK_DIM = RET_V_DIM // 2
RET_CHUNK = 128
RET_THETA = 10000.0

ROPE_THETA = 500000.0
ROPE_FRACTION = 4
EPS = 1e-6

LANES = 128
COL_TILE = 512
NEG = -1e30
VMEM_LIMIT = 56 * 1024 * 1024

_NT = (((1,), (1,)), ((), ()))
_TN = (((0,), (0,)), ((), ()))


def _cparams(n_axes):
    return pltpu.CompilerParams(dimension_semantics=("arbitrary",) * n_axes,
                                vmem_limit_bytes=VMEM_LIMIT)


def _tile_sizes(m, seq):
    return {
        "in_rows": (min(512, m), min(256, m), min(512, m)),
        "out_rows": min(512, seq),
        "moba_blocks": min(4, seq // MOBA_BLOCK),
        "ret_rows": min(1024, seq),
    }


def _silu(z):
    return z * (1.0 / (1.0 + jnp.exp(-z)))


def _weight_stream(w_hbm, w_scr, stage, sem, slices, axis):
    def window(ref, sl, lead=()):
        idx = (sl, slice(None)) if axis == 0 else (slice(None), sl)
        return ref.at[lead + idx]

    def copy(i):
        sl = slices[i]
        local = slice(0, sl.stop - sl.start)
        return pltpu.make_async_copy(window(w_hbm, sl), window(stage, local, (i % 2,)),
                                     sem.at[i % 2])

    def start():
        copy(0).start()

    def fetch(i):
        if i + 1 < len(slices):
            copy(i + 1).start()
        copy(i).wait()
        sl = slices[i]
        local = slice(0, sl.stop - sl.start)
        idx = (sl, slice(None)) if axis == 0 else (slice(None), sl)
        loc = (local, slice(None)) if axis == 0 else (slice(None), local)
        w_scr[idx] = stage[(i % 2,) + loc].astype(BF16)

    return start, fetch


def _in_proj_kernel(x_ref, g_ref, w_hbm, pos_ref, lane_ref, ob_ref, oz_ref, *rest, plan, half,
                    rs_scale):
    qt_ref = rest[0] if len(rest) == 5 else None
    h_scr, w_ref, stage, sem = rest[-4:]
    gpt = COL_TILE // LANES
    n_grp = len(plan)
    tiles = [(g0, min(g0 + gpt, n_grp)) for g0 in range(0, n_grp, gpt)]
    tiles.sort(key=lambda t: 0 if all(dest is None for _, dest in plan[t[0]:t[1]]) else 1)
    gpp = stage.shape[2] // LANES
    pieces = [(p0, min(p0 + gpp, g1)) for g0, g1 in tiles for p0 in range(g0, g1, gpp)]
    start, fetch_piece = _weight_stream(
        w_hbm, w_ref, stage, sem, [slice(p0 * LANES, p1 * LANES) for p0, p1 in pieces], axis=1)

    def fetch(ti):
        for pi, (p0, _) in enumerate(pieces):
            if tiles[ti][0] <= p0 < tiles[ti][1]:
                fetch_piece(pi)

    def body(first_step):
        if first_step:
            start()
        x = x_ref[...]
        ms = jnp.mean(x * x, axis=-1, keepdims=True)
        h_scr[...] = ((x * lax.rsqrt(ms + EPS)) * g_ref[...]).astype(BF16)

        tables = {}
        anchor = []

        def rope(a, pat):
            if 0 not in tables:
                ang = pos_ref[...] * lane_ref[0:1, :] + anchor[0] * 0.0
                sn = jnp.sin(ang)
                tables[0] = (jnp.cos(ang), sn * lane_ref[1:2, :], sn * lane_ref[2:3, :])
            if pat not in tables:
                first = lax.broadcasted_iota(jnp.int32, tables[0][0].shape, 1) < LANES // 2
                c0, sp0, sm0 = tables[0]
                tables[pat] = (jnp.where(first, c0, 1.0), jnp.where(first, sp0, 0.0),
                               jnp.where(first, sm0, 0.0))
            c, s_plus, s_minus = tables[pat]
            out = a * c + pltpu.roll(a, half, 1) * s_plus
            if 2 * half != LANES:
                out = out + pltpu.roll(a, LANES - half, 1) * s_minus
            return out

        for ti, (g0, g1) in enumerate(tiles):
            if first_step:
                fetch(ti)
            acc = jnp.dot(h_scr[...], w_ref[:, g0 * LANES:g1 * LANES],
                          preferred_element_type=F32)
            if not anchor:
                anchor.append(acc[:, :LANES])
            for gi in range(g0, g1):
                kind, dest = plan[gi]
                a = acc[:, (gi - g0) * LANES:(gi - g0 + 1) * LANES]
                if dest is None:
                    zc = (gi - (n_grp - oz_ref.shape[1] // LANES)) * LANES
                    oz_ref[:, zc:zc + LANES] = a
                    continue
                if kind in ("R", "RS", "RST", "Rh"):
                    a = rope(a, 1 if kind == "Rh" else 0)
                if kind in ("RS", "RST"):
                    a = a * rs_scale
                if kind in ("RST", "PT"):
                    qt_ref[dest * LANES:(dest + 1) * LANES, :] = a.T.astype(BF16)
                else:
                    ob_ref[:, dest * LANES:(dest + 1) * LANES] = a.astype(BF16)

    @pl.when(pl.program_id(0) == 0)
    def _():
        body(True)

    @pl.when(pl.program_id(0) != 0)
    def _():
        body(False)


def _rope_lane_rows(head_dim, rot_dim, theta):
    half = rot_dim // 2
    inv = theta ** (-jnp.arange(0, rot_dim, 2, dtype=F32) / rot_dim)
    rest = head_dim - rot_dim
    z_half, z_rest = jnp.zeros((half,), F32), jnp.zeros((rest,), F32)
    ones = jnp.ones((half,), F32)
    inv_h = jnp.concatenate([inv, inv, z_rest])
    plus_h = jnp.concatenate([z_half, ones, z_rest])
    minus_h = jnp.concatenate([-ones, z_half, z_rest])
    if 2 * half == LANES:
        plus_h, minus_h = plus_h + minus_h, jnp.zeros_like(minus_h)
    rep = LANES // head_dim
    return jnp.stack([jnp.tile(r, rep) for r in (inv_h, plus_h, minus_h)])


def _in_proj(x2d, g, w_in, pos_col, lane_rows, plan, half, rs_scale, tm):
    m, d = x2d.shape
    n_tot = w_in.shape[1]
    n_bf = (max(dest for kind, dest in plan if dest is not None and kind[-1] != "T") + 1) * LANES
    n_qt = sum(kind[-1] == "T" for kind, _ in plan) * LANES
    row = lambda i: (i, 0)
    fixed = lambda i: (0, 0)
    out_specs = [pl.BlockSpec((tm, n_bf), row), pl.BlockSpec((tm, BRANCH_WIDTH), row)]
    out_shape = [jax.ShapeDtypeStruct((m, n_bf), BF16), jax.ShapeDtypeStruct((m, BRANCH_WIDTH), F32)]
    if n_qt:
        out_specs.append(pl.BlockSpec((n_qt, tm), lambda i: (0, i)))
        out_shape.append(jax.ShapeDtypeStruct((n_qt, m), BF16))
    return pl.pallas_call(
        functools.partial(_in_proj_kernel, plan=tuple(plan), half=half, rs_scale=rs_scale),
        grid=(m // tm,),
        in_specs=[
            pl.BlockSpec((tm, d), row),
            pl.BlockSpec((1, d), fixed),
            pl.BlockSpec(memory_space=pl.ANY),
            pl.BlockSpec((tm, 1), row),
            pl.BlockSpec(lane_rows.shape, fixed),
        ],
        out_specs=out_specs,
        out_shape=out_shape,
        scratch_shapes=[pltpu.VMEM((tm, d), BF16),
                        pltpu.VMEM((d, n_tot), BF16),
                        pltpu.VMEM((2, d, COL_TILE // 2), F32),
                        pltpu.SemaphoreType.DMA((2,))],
        compiler_params=_cparams(1),
        name="in_proj",
    )(x2d, g.reshape(1, d), w_in, pos_col, lane_rows)


def _mem_kv_kernel(x_ref, g_ref, w_ref, o_ref):
    x = x_ref[...]
    ms = jnp.mean(x * x, axis=-1, keepdims=True)
    h = ((x * lax.rsqrt(ms + EPS)) * g_ref[...]).astype(BF16)
    o_ref[...] = jnp.dot(h, w_ref[...], preferred_element_type=F32).astype(BF16)


def _mem_kv(mem2d, g, w_bf16):
    m, d = mem2d.shape
    n = w_bf16.shape[1]
    tm = min(m, 256)
    return pl.pallas_call(
        _mem_kv_kernel,
        grid=(m // tm,),
        in_specs=[
            pl.BlockSpec((tm, d), lambda i: (i, 0)),
            pl.BlockSpec((1, d), lambda i: (0, 0)),
            pl.BlockSpec((d, n), lambda i: (0, 0)),
        ],
        out_specs=pl.BlockSpec((tm, n), lambda i: (i, 0)),
        out_shape=jax.ShapeDtypeStruct((m, n), BF16),
        compiler_params=_cparams(1),
        name="mem_kv",
    )(mem2d, g.reshape(1, d), w_bf16)


SWA_Q_SCALE = SWA_HEAD_DIM ** -0.5 * math.log2(math.e)


SWA_SUB = 4


def _swa_kernel(sink_ref, qt_ref, ka_ref, kb_ref, kc_ref, o_ref, vt, s_scr, *, seq):
    w = SWA_WINDOW
    dh = SWA_HEAD_DIM
    g_per = SWA_Q_HEADS // SWA_KV_HEADS
    step = pl.program_id(1)

    @pl.when(step == 0)
    def _():
        for j in range(seq // w):
            rows = slice(j * w, (j + 1) * w)
            both = jnp.concatenate([kb_ref[rows, :], kc_ref[rows, :]], axis=1)
            vt[j] = both.astype(F32).T.astype(BF16)

    zeros = jnp.zeros((dh, g_per * w), BF16)
    vt_wins = []
    for sub in range(SWA_SUB):
        t = step * SWA_SUB + sub
        j0 = jnp.maximum(t - 1, 0)
        r0 = pl.multiple_of(j0 * w, w)
        k_wins = (ka_ref[pl.ds(r0, 2 * w), :], kb_ref[pl.ds(r0, 2 * w), :])
        vt_wins.append(jnp.concatenate([vt[j0], vt[j0 + 1]], axis=1))
        kpos = r0 + lax.broadcasted_iota(jnp.int32, (2 * w, w), 0)
        qpos = t * w + lax.broadcasted_iota(jnp.int32, (2 * w, w), 1)
        bias = jnp.where((kpos <= qpos) & (kpos > qpos - w), 0.0, NEG)
        bias = jnp.concatenate([bias] * g_per, axis=1)
        for h in range(SWA_KV_HEADS):
            q_grp = jnp.concatenate(
                [qt_ref[(h * g_per + g) * dh:(h * g_per + g + 1) * dh, sub * w:(sub + 1) * w]
                 for g in range(g_per)], axis=1)
            q_pad = jnp.concatenate([q_grp, zeros] if h % 2 == 0 else [zeros, q_grp], axis=0)
            s_scr[sub * SWA_KV_HEADS + h] = (
                jnp.dot(k_wins[h // 2], q_pad, preferred_element_type=F32) + bias)
    for sub in range(SWA_SUB):
        for h in range(SWA_KV_HEADS):
            s = s_scr[sub * SWA_KV_HEADS + h]
            sink = sink_ref[h:h + 1, :]
            mx = jnp.maximum(jnp.max(s, axis=0, keepdims=True), sink)
            p = jnp.exp2(s - mx)
            denom = jnp.sum(p, axis=0, keepdims=True) + jnp.exp2(sink - mx)
            o_t = jnp.dot(vt_wins[sub][(h + 1) * dh:(h + 2) * dh, :], p.astype(BF16),
                          preferred_element_type=F32) / denom
            for pair in range(g_per // 2):
                two = jnp.concatenate([o_t[:, (2 * pair) * w:(2 * pair + 1) * w],
                                       o_t[:, (2 * pair + 1) * w:(2 * pair + 2) * w]], axis=0)
                c0 = (h * g_per + 2 * pair) * dh
                o_ref[sub * w:(sub + 1) * w, c0:c0 + 2 * dh] = two.T.astype(BF16)


def _swa(qt, ob, sinks, batch, seq):
    m = batch * seq
    w = SWA_WINDOW
    nq = SWA_Q_HEADS * SWA_HEAD_DIM
    g_per = SWA_Q_HEADS // SWA_KV_HEADS
    n_t = seq // (SWA_SUB * w)
    kv_grp = MEM_WIDTH // LANES
    sink_rows = jnp.repeat(sinks.astype(F32) * math.log2(math.e), w).reshape(SWA_KV_HEADS, g_per * w)
    kv_spec = lambda j: pl.BlockSpec((seq, LANES), lambda b, t: (b, kv_grp + j))
    return pl.pallas_call(
        functools.partial(_swa_kernel, seq=seq),
        grid=(batch, n_t),
        in_specs=[
            pl.BlockSpec((SWA_KV_HEADS, g_per * w), lambda b, t: (0, 0)),
            pl.BlockSpec((nq, SWA_SUB * w), lambda b, t: (0, b * n_t + t)),
            kv_spec(0), kv_spec(1), kv_spec(2),
        ],
        out_specs=pl.BlockSpec((SWA_SUB * w, nq), lambda b, t: (b * n_t + t, 0)),
        out_shape=jax.ShapeDtypeStruct((m, nq), BF16),
        scratch_shapes=[pltpu.VMEM((seq // w, 2 * LANES, w), BF16),
                        pltpu.VMEM((SWA_SUB * SWA_KV_HEADS, 2 * w, g_per * w), F32)],
        compiler_params=_cparams(2),
        name="swa",
    )(sink_rows, qt, ob, ob, ob)


MOBA_BIAS_ROWS = 16
MOBA_Q_SCALE = MOBA_HEAD_DIM ** -0.5 * math.log2(math.e)


def _moba_kernel(qt_ref, k_ref, vt, o_ref, kaug, kmean, qaug, s_a, s_b, *, seq, cb, qb):
    blk = MOBA_BLOCK
    dh = MOBA_HEAD_DIM
    nblk = seq // blk
    kc = cb * blk
    wq = qb * blk
    nbr = MOBA_BIAS_ROWS
    i = pl.program_id(2)

    @pl.when(i == 0)
    def _():
        k = k_ref[...]
        kaug[:, :dh] = k
        row_blk = lax.broadcasted_iota(jnp.int32, (seq, LANES), 0) // blk
        lane = lax.broadcasted_iota(jnp.int32, (seq, LANES), 1)
        kaug[:, dh:] = (row_blk == lane).astype(BF16)
        kmean[...] = jnp.zeros_like(kmean)
        kmean[:nblk, :] = jnp.mean(k.astype(F32).reshape(nblk, blk, dh), axis=1)
        qaug[dh + nbr:, :] = jnp.zeros((dh - nbr, wq), BF16)

    q_t = qt_ref[...]
    gate = jnp.dot(kmean[...].astype(BF16), q_t, preferred_element_type=F32)
    row = lax.broadcasted_iota(jnp.int32, (nbr, wq), 0)
    rowf = row.astype(F32)
    own = i * qb + lax.broadcasted_iota(jnp.int32, (nbr, wq), 1) // blk
    past = row < own
    g = jnp.where(past, gate, -jnp.inf)
    sel = row == own
    for _ in range(min(MOBA_TOPK, nblk - 1)):
        mx = jnp.max(g, axis=0, keepdims=True)
        idx = jnp.min(jnp.where(g == mx, rowf, float(nbr)), axis=0, keepdims=True)
        hit = rowf == idx
        sel = sel | (hit & past)
        g = jnp.where(hit, -jnp.inf, g)
    qaug[:dh, :] = q_t
    qaug[dh:dh + nbr, :] = jnp.where(sel, 0.0, NEG).astype(BF16)

    c_own = (i * qb) // cb
    qpos = i * wq + lax.broadcasted_iota(jnp.int32, (kc, wq), 1)
    for case in range(nblk // cb):
        @pl.when(c_own == case)
        def _(case=case):
            order = [case] + list(range(case))
            s_bufs = (s_a, s_b)

            def stage_scores(idx):
                c = order[idx]
                s = jnp.dot(kaug[c * kc:(c + 1) * kc, :], qaug[...],
                            preferred_element_type=F32)
                if c == case:
                    kpos = c * kc + lax.broadcasted_iota(jnp.int32, (kc, wq), 0)
                    s = jnp.where(kpos <= qpos, s, NEG)
                s_bufs[idx % 2][...] = s

            stage_scores(0)
            m_col = l_col = acc = None
            for idx, c in enumerate(order):
                if idx + 1 < len(order):
                    stage_scores(idx + 1)
                rows = slice(c * kc, (c + 1) * kc)
                s = s_bufs[idx % 2][...]
                m_c = jnp.max(s, axis=0, keepdims=True)
                if m_col is None:
                    m_col = m_c
                    p = jnp.exp2(s - m_col)
                    l_col = jnp.sum(p, axis=0, keepdims=True)
                    acc = jnp.dot(vt[:, rows], p.astype(BF16), preferred_element_type=F32)
                else:
                    m_new = jnp.maximum(m_col, m_c)
                    alpha = jnp.exp2(m_col - m_new)
                    p = jnp.exp2(s - m_new)
                    l_col = alpha * l_col + jnp.sum(p, axis=0, keepdims=True)
                    acc = alpha * acc + jnp.dot(vt[:, rows], p.astype(BF16),
                                                preferred_element_type=F32)
                    m_col = m_new
            o_ref[...] = (acc / l_col).T.astype(BF16)


def _moba(qvt, ob, batch, seq, cb, qb):
    m = batch * seq
    blk = MOBA_BLOCK
    dh = MOBA_HEAD_DIM
    nh = MOBA_HEADS
    n_q = seq // (qb * blk)
    wq = qb * blk
    return pl.pallas_call(
        functools.partial(_moba_kernel, seq=seq, cb=cb, qb=qb),
        grid=(batch, nh, n_q),
        in_specs=[
            pl.BlockSpec((dh, wq), lambda b, h, i: (h, b * n_q + i)),
            pl.BlockSpec((seq, dh), lambda b, h, i: (b, h)),
            pl.BlockSpec((dh, seq), lambda b, h, i: (nh + h, b)),
        ],
        out_specs=pl.BlockSpec((wq, dh), lambda b, h, i: (b * n_q + i, h)),
        out_shape=jax.ShapeDtypeStruct((m, nh * dh), BF16),
        scratch_shapes=[pltpu.VMEM((seq, 2 * dh), BF16),
                        pltpu.VMEM((MOBA_BIAS_ROWS, dh), F32),
                        pltpu.VMEM((2 * dh, wq), BF16),
                        pltpu.VMEM((cb * blk, wq), F32),
                        pltpu.VMEM((cb * blk, wq), F32)],
        compiler_params=_cparams(3),
        name="moba",
    )(qvt, ob, qvt)


def _ret_kernel(qk_ref, v_ref, dec_ref, xi_ref, zeta_ref, o_ref, r_scr, *, n_chunk, g_chunk):
    t = RET_CHUNK
    nh, dk, dv = RET_HEADS, RET_QK_DIM, RET_V_DIM

    @pl.when(pl.program_id(1) == 0)
    def _():
        r_scr[...] = jnp.zeros_like(r_scr)

    def body(c, carry):
        r0 = pl.multiple_of(c * t, t)
        for h in range(nh):
            q = qk_ref[pl.ds(r0, t), h * dk:(h + 1) * dk]
            k = qk_ref[pl.ds(r0, t), (nh + h) * dk:(nh + h + 1) * dk]
            v = v_ref[pl.ds(r0, t), h * dv:(h + 1) * dv]
            s = lax.dot_general(q, k, _NT, preferred_element_type=F32) * dec_ref[h]
            inner = jnp.dot(s.astype(BF16), v, preferred_element_type=F32)
            r_prev = r_scr[h]
            q_x = (q.astype(F32) * xi_ref[h]).astype(BF16)
            cross = jnp.dot(q_x, r_prev.astype(BF16), preferred_element_type=F32)
            o = inner + cross
            o = o * lax.rsqrt(jnp.mean(o * o, axis=-1, keepdims=True) + EPS)
            o_ref[pl.ds(r0, t), h * dv:(h + 1) * dv] = o.astype(BF16)
            k_z = (k.astype(F32) * zeta_ref[h]).astype(BF16)
            kv = lax.dot_general(k_z, v, _TN, preferred_element_type=F32)
            r_scr[h] = g_chunk[h] * r_prev + kv
        return carry

    lax.fori_loop(0, n_chunk, body, 0, unroll=True)


def _ret_constants():
    nh, t, dk = RET_HEADS, RET_CHUNK, RET_QK_DIM
    lin = [math.log(1.0 / 32) + (math.log(1.0 / 512) - math.log(1.0 / 32)) * h / (nh - 1)
           for h in range(nh)]
    log_g = [math.log1p(-math.exp(v)) for v in lin]
    i = jnp.arange(t, dtype=F32)
    lg = jnp.asarray(log_g, F32)
    diff = i[:, None] - i[None, :]
    decay = jnp.where(diff >= 0, jnp.exp(jnp.maximum(diff, 0.0)[None] * lg[:, None, None]), 0.0)
    xi = jnp.exp((i + 1)[None, :] * lg[:, None])
    zeta = jnp.exp((t - 1 - i)[None, :] * lg[:, None])
    xi_t = jnp.broadcast_to(xi[:, :, None], (nh, t, dk))
    zeta_t = jnp.broadcast_to(zeta[:, :, None], (nh, t, dk))
    g_chunk = tuple(math.exp(t * v) for v in log_g)
    return decay, xi_t, zeta_t, g_chunk


def _ret(ob, batch, seq, seg):
    m = batch * seq
    nh, dk, dv, t = RET_HEADS, RET_QK_DIM, RET_V_DIM, RET_CHUNK
    n_seg = seq // seg
    decay, xi_t, zeta_t, g_chunk = _ret_constants()
    wqk = 2 * nh * dk
    wv = nh * dv
    const = lambda b, s: (0, 0, 0)
    return pl.pallas_call(
        functools.partial(_ret_kernel, n_chunk=seg // t, g_chunk=g_chunk),
        grid=(batch, n_seg),
        in_specs=[
            pl.BlockSpec((seg, wqk), lambda b, s: (b * n_seg + s, 0)),
            pl.BlockSpec((seg, wv), lambda b, s: (b * n_seg + s, wqk // wv)),
            pl.BlockSpec((nh, t, t), const),
            pl.BlockSpec((nh, t, dk), const),
            pl.BlockSpec((nh, t, dk), const),
        ],
        out_specs=pl.BlockSpec((seg, wv), lambda b, s: (b * n_seg + s, 0)),
        out_shape=jax.ShapeDtypeStruct((m, wv), BF16),
        scratch_shapes=[pltpu.VMEM((nh, dk, dv), F32)],
        compiler_params=_cparams(2),
        name="retention",
    )(ob, ob, decay, xi_t, zeta_t)


def _out_kernel(mix_ref, qm_ref, z_ref, x_ref, mk_ref, mv_ref, w_hbm, fn_ref, o_ref, y_scr, w_ref,
                stage, sem, *, final):
    dm = MEM_HEAD_DIM
    scale = dm ** -0.5
    n_slices = BRANCH_WIDTH // COL_TILE
    rows = stage.shape[1]
    per = COL_TILE // rows
    start, fetch_piece = _weight_stream(
        w_hbm, w_ref, stage, sem,
        [slice(j * rows, (j + 1) * rows) for j in range(BRANCH_WIDTH // rows)], axis=0)

    def fetch(t):
        for j in range(per):
            fetch_piece(t * per + j)

    def body(first_step):
        if first_step:
            start()
        out = x_ref[...]
        for t in range(MIX_WIDTH // COL_TILE):
            lo, hi = t * COL_TILE, (t + 1) * COL_TILE
            y_t = (mix_ref[:, lo:hi].astype(F32) * _silu(z_ref[:, lo:hi])).astype(BF16)
            if first_step:
                fetch(t)
            out = out + jnp.dot(y_t, w_ref[lo:hi, :], preferred_element_type=F32)
        for h in range(MEM_HEADS):
            lo, hi = h * dm, (h + 1) * dm
            s = lax.dot_general(qm_ref[:, lo:hi], mk_ref[:, lo:hi], _NT,
                                preferred_element_type=F32) * scale
            p = jnp.exp(s - jnp.max(s, axis=-1, keepdims=True))
            l = jnp.sum(p, axis=-1, keepdims=True)
            o = jnp.dot(p.astype(BF16), mv_ref[:, lo:hi], preferred_element_type=F32) / l
            y_scr[:, lo:hi] = (o * _silu(z_ref[:, MIX_WIDTH + lo:MIX_WIDTH + hi])).astype(BF16)
        if first_step:
            fetch(n_slices - 1)
        out = out + jnp.dot(y_scr[...], w_ref[MIX_WIDTH:, :], preferred_element_type=F32)
        if final:
            ms = jnp.mean(out * out, axis=-1, keepdims=True)
            out = (out * lax.rsqrt(ms + EPS)) * fn_ref[...]
        o_ref[...] = out

    @pl.when(pl.program_id(0) == 0)
    def _():
        body(True)

    @pl.when(pl.program_id(0) != 0)
    def _():
        body(False)


def _out_proj(mix, ob, qm_blk, z, x2d, mkv, w_out, final_norm, batch, seq, tm, final):
    m, d = x2d.shape
    per_b = seq // tm
    row = lambda i: (i, 0)
    return pl.pallas_call(
        functools.partial(_out_kernel, final=final),
        grid=(m // tm,),
        in_specs=[
            pl.BlockSpec((tm, MIX_WIDTH), row),
            pl.BlockSpec((tm, MEM_WIDTH), lambda i: (i, qm_blk)),
            pl.BlockSpec((tm, BRANCH_WIDTH), row),
            pl.BlockSpec((tm, d), row),
            pl.BlockSpec((N_MEM, MEM_WIDTH), lambda i: (i // per_b, 0)),
            pl.BlockSpec((N_MEM, MEM_WIDTH), lambda i: (i // per_b, 1)),
            pl.BlockSpec(memory_space=pl.ANY),
            pl.BlockSpec((1, d), lambda i: (0, 0)),
        ],
        out_specs=pl.BlockSpec((tm, d), row),
        out_shape=jax.ShapeDtypeStruct((m, d), F32),
        scratch_shapes=[pltpu.VMEM((tm, MEM_WIDTH), BF16),
                        pltpu.VMEM((BRANCH_WIDTH, d), BF16),
                        pltpu.VMEM((2, COL_TILE // 2, d), F32),
                        pltpu.SemaphoreType.DMA((2,))],
        compiler_params=_cparams(1),
        name="out_proj",
    )(mix, ob, z, x2d, mkv, mkv, w_out, final_norm.reshape(1, d))


def _in_proj_plan(mixer):
    n_qm = MEM_WIDTH // LANES
    n_z = BRANCH_WIDTH // LANES
    if mixer == 0:
        n_q = SWA_Q_HEADS * SWA_HEAD_DIM // LANES
        plan = [("RST", gq) for gq in range(n_q)]
        plan += [("R", n_qm), ("Rh", n_qm + 1), ("P", n_qm + 2)]
        plan += [("P", j) for j in range(n_qm)]
    elif mixer == 1:
        n_h = MIX_WIDTH // LANES
        plan = [("RST", j) for j in range(n_h)] + [("R", j) for j in range(n_h)]
        plan += [("PT", n_h + j) for j in range(n_h)] + [("P", n_h + j) for j in range(n_qm)]
    else:
        n_qk = RET_HEADS * RET_QK_DIM // LANES
        kinds = ["R"] * n_qk + ["RS"] * n_qk + ["P"] * (MIX_WIDTH // LANES + n_qm)
        plan = [(kind, j) for j, kind in enumerate(kinds)]
    return plan + [("P", None)] * n_z


def kernel(x, mem, positions, mem_norm, w_mem_kv, norm_0, w_in_0, sinks_0, w_out_0, norm_1, w_in_1,
           w_out_1, norm_2, w_in_2, w_out_2, norm_3, w_in_3, sinks_3, w_out_3, final_norm):
    batch, seq, d = x.shape
    m = batch * seq
    layers = [(norm_0, w_in_0, w_out_0, sinks_0), (norm_1, w_in_1, w_out_1, None),
              (norm_2, w_in_2, w_out_2, None), (norm_3, w_in_3, w_out_3, sinks_3)]

    mkv = _mem_kv(mem.reshape(batch * N_MEM, d), mem_norm, w_mem_kv.astype(BF16))

    pos_col = positions.astype(F32).reshape(m, 1)
    rope = (
        (_rope_lane_rows(SWA_HEAD_DIM, SWA_HEAD_DIM // ROPE_FRACTION, ROPE_THETA),
         SWA_HEAD_DIM // ROPE_FRACTION // 2),
        (_rope_lane_rows(MOBA_HEAD_DIM, MOBA_HEAD_DIM // ROPE_FRACTION, ROPE_THETA),
         MOBA_HEAD_DIM // ROPE_FRACTION // 2),
        (_rope_lane_rows(RET_QK_DIM, RET_QK_DIM, RET_THETA), RET_QK_DIM // 2),
    )

    tiles = _tile_sizes(m, seq)
    h = x.reshape(m, d)
    n_layers = len(layers)
    for li, (g, w_in, w_out, sinks) in enumerate(layers):
        mixer = li % N_MIXERS
        lane_rows, half = rope[mixer]
        rs_scale = (SWA_Q_SCALE, MOBA_Q_SCALE, RET_QK_DIM ** -0.5)[mixer]
        ob, z, *qt = _in_proj(h, g, w_in, pos_col, lane_rows, _in_proj_plan(mixer), half,
                              rs_scale, tiles["in_rows"][mixer])
        if mixer == 0:
            mix = _swa(qt[0], ob, sinks, batch, seq)
            qm_blk = 0
        elif mixer == 1:
            mix = _moba(qt[0], ob, batch, seq, cb=tiles["moba_blocks"], qb=tiles["moba_blocks"])
            qm_blk = MIX_WIDTH // MEM_WIDTH
        else:
            mix = _ret(ob, batch, seq, seg=tiles["ret_rows"])
            qm_blk = (2 * RET_HEADS * RET_QK_DIM + MIX_WIDTH) // MEM_WIDTH
        h = _out_proj(mix, ob, qm_blk, z, h, mkv, w_out, final_norm, batch, seq,
                      tiles["out_rows"], final=(li == n_layers - 1))
    return h.reshape(batch, seq, d)
```

```python
import functools
import math

import jax
import jax.numpy as jnp
from jax import lax
from jax.experimental import pallas as pl
from jax.experimental.pallas import tpu as pltpu

F32 = jnp.float32
BF16 = jnp.bfloat16

D_MODEL = 2048
N_MEM = 256
N_MIXERS = 3
BRANCH_WIDTH = D_MODEL
MEM_HEADS = 4
MEM_HEAD_DIM = 128
MEM_WIDTH = MEM_HEADS * MEM_HEAD_DIM
MIX_WIDTH = BRANCH_WIDTH - MEM_WIDTH

SWA_HEAD_DIM = 64
SWA_Q_HEADS = MIX_WIDTH // SWA_HEAD_DIM
SWA_KV_HEADS = SWA_Q_HEADS // 8
SWA_WINDOW = 128

MOBA_HEAD_DIM = 128
MOBA_HEADS = MIX_WIDTH // MOBA_HEAD_DIM
MOBA_BLOCK = 256
MOBA_TOPK = 3

RET_HEADS = 6
RET_V_DIM = MIX_WIDTH // RET_HEADS
RET_QK_DIM = RET_V_DIM // 2
RET_CHUNK = 128
RET_THETA = 10000.0

ROPE_THETA = 500000.0
ROPE_FRACTION = 4
EPS = 1e-6

LANES = 128
COL_TILE = 512
NEG = -1e30
VMEM_LIMIT = 56 * 1024 * 1024

_NT = (((1,), (1,)), ((), ()))
_TN = (((0,), (0,)), ((), ()))


def _cparams(n_axes):
    return pltpu.CompilerParams(dimension_semantics=("arbitrary",) * n_axes,
                                vmem_limit_bytes=VMEM_LIMIT)


def _tile_sizes(m, seq):
    return {
        "in_rows": (min(512, m), min(256, m), min(512, m)),
        "out_rows": min(512, seq),
        "moba_blocks": min(4, seq // MOBA_BLOCK),
        "ret_rows": min(1024, seq),
    }


def _silu(z):
    return z * (1.0 / (1.0 + jnp.exp(-z)))


def _weight_stream(w_hbm, w_scr, stage, sem, slices, axis):
    def window(ref, sl, lead=()):
        idx = (sl, slice(None)) if axis == 0 else (slice(None), sl)
        return ref.at[lead + idx]

    def copy(i):
        sl = slices[i]
        local = slice(0, sl.stop - sl.start)
        return pltpu.make_async_copy(window(w_hbm, sl), window(stage, local, (i % 2,)),
                                     sem.at[i % 2])

    def start():
        copy(0).start()

    def fetch(i):
        if i + 1 < len(slices):
            copy(i + 1).start()
        copy(i).wait()
        sl = slices[i]
        local = slice(0, sl.stop - sl.start)
        idx = (sl, slice(None)) if axis == 0 else (slice(None), sl)
        loc = (local, slice(None)) if axis == 0 else (slice(None), local)
        w_scr[idx] = stage[(i % 2,) + loc].astype(BF16)

    return start, fetch


def _in_proj_kernel(x_ref, g_ref, w_hbm, pos_ref, lane_ref, ob_ref, oz_ref, *rest, plan, half,
                    rs_scale):
    qt_ref = rest[0] if len(rest) == 5 else None
    h_scr, w_ref, stage, sem = rest[-4:]
    gpt = COL_TILE // LANES
    n_grp = len(plan)
    tiles = [(g0, min(g0 + gpt, n_grp)) for g0 in range(0, n_grp, gpt)]
    tiles.sort(key=lambda t: 0 if all(dest is None for _, dest in plan[t[0]:t[1]]) else 1)
    gpp = stage.shape[2] // LANES
    pieces = [(p0, min(p0 + gpp, g1)) for g0, g1 in tiles for p0 in range(g0, g1, gpp)]
    start, fetch_piece = _weight_stream(
        w_hbm, w_ref, stage, sem, [slice(p0 * LANES, p1 * LANES) for p0, p1 in pieces], axis=1)

    def fetch(ti):
        for pi, (p0, _) in enumerate(pieces):
            if tiles[ti][0] <= p0 < tiles[ti][1]:
                fetch_piece(pi)

    def body(first_step):
        if first_step:
            start()
        x = x_ref[...]
        ms = jnp.mean(x * x, axis=-1, keepdims=True)
        h_scr[...] = ((x * lax.rsqrt(ms + EPS)) * g_ref[...]).astype(BF16)

        tables = {}
        anchor = []

        def rope(a, pat):
            if 0 not in tables:
                ang = pos_ref[...] * lane_ref[0:1, :] + anchor[0] * 0.0
                sn = jnp.sin(ang)
                tables[0] = (jnp.cos(ang), sn * lane_ref[1:2, :], sn * lane_ref[2:3, :])
            if pat not in tables:
                first = lax.broadcasted_iota(jnp.int32, tables[0][0].shape, 1) < LANES // 2
                c0, sp0, sm0 = tables[0]
                tables[pat] = (jnp.where(first, c0, 1.0), jnp.where(first, sp0, 0.0),
                               jnp.where(first, sm0, 0.0))
            c, s_plus, s_minus = tables[pat]
            out = a * c + pltpu.roll(a, half, 1) * s_plus
            if 2 * half != LANES:
                out = out + pltpu.roll(a, LANES - half, 1) * s_minus
            return out

        for ti, (g0, g1) in enumerate(tiles):
            if first_step:
                fetch(ti)
            acc = jnp.dot(h_scr[...], w_ref[:, g0 * LANES:g1 * LANES],
                          preferred_element_type=F32)
            if not anchor:
                anchor.append(acc[:, :LANES])
            for gi in range(g0, g1):
                kind, dest = plan[gi]
                a = acc[:, (gi - g0) * LANES:(gi - g0 + 1) * LANES]
                if dest is None:
                    zc = (gi - (n_grp - oz_ref.shape[1] // LANES)) * LANES
                    oz_ref[:, zc:zc + LANES] = a
                    continue
                if kind in ("R", "RS", "RST", "Rh"):
                    a = rope(a, 1 if kind == "Rh" else 0)
                if kind in ("RS", "RST"):
                    a = a * rs_scale
                if kind in ("RST", "PT"):
                    qt_ref[dest * LANES:(dest + 1) * LANES, :] = a.T.astype(BF16)
                else:
                    ob_ref[:, dest * LANES:(dest + 1) * LANES] = a.astype(BF16)

    @pl.when(pl.program_id(0) == 0)
    def _():
        body(True)

    @pl.when(pl.program_id(0) != 0)
    def _():
        body(False)


def _rope_lane_rows(head_dim, rot_dim, theta):
    half = rot_dim // 2
    inv = theta ** (-jnp.arange(0, rot_dim, 2, dtype=F32) / rot_dim)
    rest = head_dim - rot_dim
    z_half, z_rest = jnp.zeros((half,), F32), jnp.zeros((rest,), F32)
    ones = jnp.ones((half,), F32)
    inv_h = jnp.concatenate([inv, inv, z_rest])
    plus_h = jnp.concatenate([z_half, ones, z_rest])
    minus_h = jnp.concatenate([-ones, z_half, z_rest])
    if 2 * half == LANES:
        plus_h, minus_h = plus_h + minus_h, jnp.zeros_like(minus_h)
    rep = LANES // head_dim
    return jnp.stack([jnp.tile(r, rep) for r in (inv_h, plus_h, minus_h)])


def _in_proj(x2d, g, w_in, pos_col, lane_rows, plan, half, rs_scale, tm):
    m, d = x2d.shape
    n_tot = w_in.shape[1]
    n_bf = (max(dest for kind, dest in plan if dest is not None and kind[-1] != "T") + 1) * LANES
    n_qt = sum(kind[-1] == "T" for kind, _ in plan) * LANES
    row = lambda i: (i, 0)
    fixed = lambda i: (0, 0)
    out_specs = [pl.BlockSpec((tm, n_bf), row), pl.BlockSpec((tm, BRANCH_WIDTH), row)]
    out_shape = [jax.ShapeDtypeStruct((m, n_bf), BF16), jax.ShapeDtypeStruct((m, BRANCH_WIDTH), F32)]
    if n_qt:
        out_specs.append(pl.BlockSpec((n_qt, tm), lambda i: (0, i)))
        out_shape.append(jax.ShapeDtypeStruct((n_qt, m), BF16))
    return pl.pallas_call(
        functools.partial(_in_proj_kernel, plan=tuple(plan), half=half, rs_scale=rs_scale),
        grid=(m // tm,),
        in_specs=[
            pl.BlockSpec((tm, d), row),
            pl.BlockSpec((1, d), fixed),
            pl.BlockSpec(memory_space=pl.ANY),
            pl.BlockSpec((tm, 1), row),
            pl.BlockSpec(lane_rows.shape, fixed),
        ],
        out_specs=out_specs,
        out_shape=out_shape,
        scratch_shapes=[pltpu.VMEM((tm, d), BF16),
                        pltpu.VMEM((d, n_tot), BF16),
                        pltpu.VMEM((2, d, COL_TILE // 2), F32),
                        pltpu.SemaphoreType.DMA((2,))],
        compiler_params=_cparams(1),
        name="in_proj",
    )(x2d, g.reshape(1, d), w_in, pos_col, lane_rows)


def _mem_kv_kernel(x_ref, g_ref, w_ref, o_ref):
    x = x_ref[...]
    ms = jnp.mean(x * x, axis=-1, keepdims=True)
    h = ((x * lax.rsqrt(ms + EPS)) * g_ref[...]).astype(BF16)
    o_ref[...] = jnp.dot(h, w_ref[...], preferred_element_type=F32).astype(BF16)


def _mem_kv(mem2d, g, w_bf16):
    m, d = mem2d.shape
    n = w_bf16.shape[1]
    tm = min(m, 256)
    return pl.pallas_call(
        _mem_kv_kernel,
        grid=(m // tm,),
        in_specs=[
            pl.BlockSpec((tm, d), lambda i: (i, 0)),
            pl.BlockSpec((1, d), lambda i: (0, 0)),
            pl.BlockSpec((d, n), lambda i: (0, 0)),
        ],
        out_specs=pl.BlockSpec((tm, n), lambda i: (i, 0)),
        out_shape=jax.ShapeDtypeStruct((m, n), BF16),
        compiler_params=_cparams(1),
        name="mem_kv",
    )(mem2d, g.reshape(1, d), w_bf16)


SWA_Q_SCALE = SWA_HEAD_DIM ** -0.5 * math.log2(math.e)


SWA_SUB = 4


def _swa_kernel(sink_ref, qt_ref, ka_ref, kb_ref, kc_ref, o_ref, vt, s_scr, *, seq):
    w = SWA_WINDOW
    dh = SWA_HEAD_DIM
    g_per = SWA_Q_HEADS // SWA_KV_HEADS
    step = pl.program_id(1)

    @pl.when(step == 0)
    def _():
        for j in range(seq // w):
            rows = slice(j * w, (j + 1) * w)
            both = jnp.concatenate([kb_ref[rows, :], kc_ref[rows, :]], axis=1)
            vt[j] = both.astype(F32).T.astype(BF16)

    zeros = jnp.zeros((dh, g_per * w), BF16)
    vt_wins = []
    for sub in range(SWA_SUB):
        t = step * SWA_SUB + sub
        j0 = jnp.maximum(t - 1, 0)
        r0 = pl.multiple_of(j0 * w, w)
        k_wins = (ka_ref[pl.ds(r0, 2 * w), :], kb_ref[pl.ds(r0, 2 * w), :])
        vt_wins.append(jnp.concatenate([vt[j0], vt[j0 + 1]], axis=1))
        kpos = r0 + lax.broadcasted_iota(jnp.int32, (2 * w, w), 0)
        qpos = t * w + lax.broadcasted_iota(jnp.int32, (2 * w, w), 1)
        bias = jnp.where((kpos <= qpos) & (kpos > qpos - w), 0.0, NEG)
        bias = jnp.concatenate([bias] * g_per, axis=1)
        for h in range(SWA_KV_HEADS):
            q_grp = jnp.concatenate(
                [qt_ref[(h * g_per + g) * dh:(h * g_per + g + 1) * dh, sub * w:(sub + 1) * w]
                 for g in range(g_per)], axis=1)
            q_pad = jnp.concatenate([q_grp, zeros] if h % 2 == 0 else [zeros, q_grp], axis=0)
            s_scr[sub * SWA_KV_HEADS + h] = (
                jnp.dot(k_wins[h // 2], q_pad, preferred_element_type=F32) + bias)
    for sub in range(SWA_SUB):
        for h in range(SWA_KV_HEADS):
            s = s_scr[sub * SWA_KV_HEADS + h]
            sink = sink_ref[h:h + 1, :]
            mx = jnp.maximum(jnp.max(s, axis=0, keepdims=True), sink)
            p = jnp.exp2(s - mx)
            denom = jnp.sum(p, axis=0, keepdims=True) + jnp.exp2(sink - mx)
            o_t = jnp.dot(vt_wins[sub][(h + 1) * dh:(h + 2) * dh, :], p.astype(BF16),
                          preferred_element_type=F32) / denom
            for pair in range(g_per // 2):
                two = jnp.concatenate([o_t[:, (2 * pair) * w:(2 * pair + 1) * w],
                                       o_t[:, (2 * pair + 1) * w:(2 * pair + 2) * w]], axis=0)
                c0 = (h * g_per + 2 * pair) * dh
                o_ref[sub * w:(sub + 1) * w, c0:c0 + 2 * dh] = two.T.astype(BF16)


def _swa(qt, ob, sinks, batch, seq):
    m = batch * seq
    w = SWA_WINDOW
    nq = SWA_Q_HEADS * SWA_HEAD_DIM
    g_per = SWA_Q_HEADS // SWA_KV_HEADS
    n_t = seq // (SWA_SUB * w)
    kv_grp = MEM_WIDTH // LANES
    sink_rows = jnp.repeat(sinks.astype(F32) * math.log2(math.e), w).reshape(SWA_KV_HEADS, g_per * w)
    kv_spec = lambda j: pl.BlockSpec((seq, LANES), lambda b, t: (b, kv_grp + j))
    return pl.pallas_call(
        functools.partial(_swa_kernel, seq=seq),
        grid=(batch, n_t),
        in_specs=[
            pl.BlockSpec((SWA_KV_HEADS, g_per * w), lambda b, t: (0, 0)),
            pl.BlockSpec((nq, SWA_SUB * w), lambda b, t: (0, b * n_t + t)),
            kv_spec(0), kv_spec(1), kv_spec(2),
        ],
        out_specs=pl.BlockSpec((SWA_SUB * w, nq), lambda b, t: (b * n_t + t, 0)),
        out_shape=jax.ShapeDtypeStruct((m, nq), BF16),
        scratch_shapes=[pltpu.VMEM((seq // w, 2 * LANES, w), BF16),
                        pltpu.VMEM((SWA_SUB * SWA_KV_HEADS, 2 * w, g_per * w), F32)],
        compiler_params=_cparams(2),
        name="swa",
    )(sink_rows, qt, ob, ob, ob)


MOBA_BIAS_ROWS = 16
MOBA_Q_SCALE = MOBA_HEAD_DIM ** -0.5 * math.log2(math.e)


def _moba_kernel(qt_ref, k_ref, vt, o_ref, kaug, kmean, qaug, s_a, s_b, *, seq, cb, qb):
    blk = MOBA_BLOCK
    dh = MOBA_HEAD_DIM
    nblk = seq // blk
    kc = cb * blk
    wq = qb * blk
    nbr = MOBA_BIAS_ROWS
    i = pl.program_id(2)

    @pl.when(i == 0)
    def _():
        k = k_ref[...]
        kaug[:, :dh] = k
        row_blk = lax.broadcasted_iota(jnp.int32, (seq, LANES), 0) // blk
        lane = lax.broadcasted_iota(jnp.int32, (seq, LANES), 1)
        kaug[:, dh:] = (row_blk == lane).astype(BF16)
        kmean[...] = jnp.zeros_like(kmean)
        kmean[:nblk, :] = jnp.mean(k.astype(F32).reshape(nblk, blk, dh), axis=1)
        qaug[dh + nbr:, :] = jnp.zeros((dh - nbr, wq), BF16)

    q_t = qt_ref[...]
    gate = jnp.dot(kmean[...].astype(BF16), q_t, preferred_element_type=F32)
    row = lax.broadcasted_iota(jnp.int32, (nbr, wq), 0)
    rowf = row.astype(F32)
    own = i * qb + lax.broadcasted_iota(jnp.int32, (nbr, wq), 1) // blk
    past = row < own
    g = jnp.where(past, gate, -jnp.inf)
    sel = row == own
    for _ in range(min(MOBA_TOPK, nblk - 1)):
        mx = jnp.max(g, axis=0, keepdims=True)
        idx = jnp.min(jnp.where(g == mx, rowf, float(nbr)), axis=0, keepdims=True)
        hit = rowf == idx
        sel = sel | (hit & past)
        g = jnp.where(hit, -jnp.inf, g)
    qaug[:dh, :] = q_t
    qaug[dh:dh + nbr, :] = jnp.where(sel, 0.0, NEG).astype(BF16)

    causal = (lax.broadcasted_iota(jnp.int32, (blk, blk), 0)
              <= lax.broadcasted_iota(jnp.int32, (blk, blk), 1))
    for case in range(nblk // cb):
        @pl.when(i == case)
        def _(case=case):
            order = [case] + list(range(case))
            s_bufs = (s_a, s_b)

            def stage_scores(idx):
                c = order[idx]
                s = jnp.dot(kaug[c * kc:(c + 1) * kc, :], qaug[...],
                            preferred_element_type=F32)
                s_bufs[idx % 2][...] = s
                if c == case:
                    for d in range(qb):
                        diag = slice(d * blk, (d + 1) * blk)
                        s_bufs[idx % 2][diag, diag] = jnp.where(causal, s[diag, diag], NEG)

            stage_scores(0)
            m_col = l_col = acc = None
            for idx, c in enumerate(order):
                if idx + 1 < len(order):
                    stage_scores(idx + 1)
                rows = slice(c * kc, (c + 1) * kc)
                s = s_bufs[idx % 2][...]
                m_c = jnp.max(s, axis=0, keepdims=True)
                if m_col is None:
                    m_col = m_c
                    p = jnp.exp2(s - m_col)
                    l_col = jnp.sum(p, axis=0, keepdims=True)
                    acc = jnp.dot(vt[:, rows], p.astype(BF16), preferred_element_type=F32)
                else:
                    m_new = jnp.maximum(m_col, m_c)
                    alpha = jnp.exp2(m_col - m_new)
                    p = jnp.exp2(s - m_new)
                    l_col = alpha * l_col + jnp.sum(p, axis=0, keepdims=True)
                    acc = alpha * acc + jnp.dot(vt[:, rows], p.astype(BF16),
                                                preferred_element_type=F32)
                    m_col = m_new
            o_ref[...] = (acc / l_col).T.astype(BF16)


def _moba(qvt, ob, batch, seq, cb, qb):
    assert cb == qb
    m = batch * seq
    blk = MOBA_BLOCK
    dh = MOBA_HEAD_DIM
    nh = MOBA_HEADS
    n_q = seq // (qb * blk)
    wq = qb * blk
    return pl.pallas_call(
        functools.partial(_moba_kernel, seq=seq, cb=cb, qb=qb),
        grid=(batch, nh, n_q),
        in_specs=[
            pl.BlockSpec((dh, wq), lambda b, h, i: (h, b * n_q + i)),
            pl.BlockSpec((seq, dh), lambda b, h, i: (b, h)),
            pl.BlockSpec((dh, seq), lambda b, h, i: (nh + h, b)),
        ],
        out_specs=pl.BlockSpec((wq, dh), lambda b, h, i: (b * n_q + i, h)),
        out_shape=jax.ShapeDtypeStruct((m, nh * dh), BF16),
        scratch_shapes=[pltpu.VMEM((seq, 2 * dh), BF16),
                        pltpu.VMEM((MOBA_BIAS_ROWS, dh), F32),
                        pltpu.VMEM((2 * dh, wq), BF16),
                        pltpu.VMEM((cb * blk, wq), F32),
                        pltpu.VMEM((cb * blk, wq), F32)],
        compiler_params=_cparams(3),
        name="moba",
    )(qvt, ob, qvt)


def _ret_kernel(qk_ref, v_ref, dec_ref, xi_ref, zeta_ref, o_ref, r_scr, *, n_chunk, g_chunk):
    t = RET_CHUNK
    nh, dk, dv = RET_HEADS, RET_QK_DIM, RET_V_DIM

    @pl.when(pl.program_id(1) == 0)
    def _():
        r_scr[...] = jnp.zeros_like(r_scr)

    def body(c, carry):
        r0 = pl.multiple_of(c * t, t)
        for h in range(nh):
            q = qk_ref[pl.ds(r0, t), h * dk:(h + 1) * dk]
            k = qk_ref[pl.ds(r0, t), (nh + h) * dk:(nh + h + 1) * dk]
            v = v_ref[pl.ds(r0, t), h * dv:(h + 1) * dv]
            s = lax.dot_general(q, k, _NT, preferred_element_type=F32) * dec_ref[h]
            inner = jnp.dot(s.astype(BF16), v, preferred_element_type=F32)
            r_prev = r_scr[h]
            q_x = (q.astype(F32) * xi_ref[h]).astype(BF16)
            cross = jnp.dot(q_x, r_prev.astype(BF16), preferred_element_type=F32)
            o = inner + cross
            o = o * lax.rsqrt(jnp.mean(o * o, axis=-1, keepdims=True) + EPS)
            o_ref[pl.ds(r0, t), h * dv:(h + 1) * dv] = o.astype(BF16)
            k_z = (k.astype(F32) * zeta_ref[h]).astype(BF16)
            kv = lax.dot_general(k_z, v, _TN, preferred_element_type=F32)
            r_scr[h] = g_chunk[h] * r_prev + kv
        return carry

    lax.fori_loop(0, n_chunk, body, 0, unroll=True)


def _ret_constants():
    nh, t, dk = RET_HEADS, RET_CHUNK, RET_QK_DIM
    lin = [math.log(1.0 / 32) + (math.log(1.0 / 512) - math.log(1.0 / 32)) * h / (nh - 1)
           for h in range(nh)]
    log_g = [math.log1p(-math.exp(v)) for v in lin]
    i = jnp.arange(t, dtype=F32)
    lg = jnp.asarray(log_g, F32)
    diff = i[:, None] - i[None, :]
    decay = jnp.where(diff >= 0, jnp.exp(jnp.maximum(diff, 0.0)[None] * lg[:, None, None]), 0.0)
    xi = jnp.exp((i + 1)[None, :] * lg[:, None])
    zeta = jnp.exp((t - 1 - i)[None, :] * lg[:, None])
    xi_t = jnp.broadcast_to(xi[:, :, None], (nh, t, dk))
    zeta_t = jnp.broadcast_to(zeta[:, :, None], (nh, t, dk))
    g_chunk = tuple(math.exp(t * v) for v in log_g)
    return decay, xi_t, zeta_t, g_chunk


def _ret(ob, batch, seq, seg):
    m = batch * seq
    nh, dk, dv, t = RET_HEADS, RET_QK_DIM, RET_V_DIM, RET_CHUNK
    n_seg = seq // seg
    decay, xi_t, zeta_t, g_chunk = _ret_constants()
    wqk = 2 * nh * dk
    wv = nh * dv
    const = lambda b, s: (0, 0, 0)
    return pl.pallas_call(
        functools.partial(_ret_kernel, n_chunk=seg // t, g_chunk=g_chunk),
        grid=(batch, n_seg),
        in_specs=[
            pl.BlockSpec((seg, wqk), lambda b, s: (b * n_seg + s, 0)),
            pl.BlockSpec((seg, wv), lambda b, s: (b * n_seg + s, wqk // wv)),
            pl.BlockSpec((nh, t, t), const),
            pl.BlockSpec((nh, t, dk), const),
            pl.BlockSpec((nh, t, dk), const),
        ],
        out_specs=pl.BlockSpec((seg, wv), lambda b, s: (b * n_seg + s, 0)),
        out_shape=jax.ShapeDtypeStruct((m, wv), BF16),
        scratch_shapes=[pltpu.VMEM((nh, dk, dv), F32)],
        compiler_params=_cparams(2),
        name="retention",
    )(ob, ob, decay, xi_t, zeta_t)


def _out_kernel(mix_ref, qm_ref, z_ref, x_ref, mk_ref, mv_ref, w_hbm, fn_ref, o_ref, y_scr, w_ref,
                stage, sem, *, final):
    dm = MEM_HEAD_DIM
    scale = dm ** -0.5
    n_slices = BRANCH_WIDTH // COL_TILE
    rows = stage.shape[1]
    per = COL_TILE // rows
    start, fetch_piece = _weight_stream(
        w_hbm, w_ref, stage, sem,
        [slice(j * rows, (j + 1) * rows) for j in range(BRANCH_WIDTH // rows)], axis=0)

    def fetch(t):
        for j in range(per):
            fetch_piece(t * per + j)

    def body(first_step):
        if first_step:
            start()
        out = x_ref[...]
        for t in range(MIX_WIDTH // COL_TILE):
            lo, hi = t * COL_TILE, (t + 1) * COL_TILE
            y_t = (mix_ref[:, lo:hi].astype(F32) * _silu(z_ref[:, lo:hi])).astype(BF16)
            if first_step:
                fetch(t)
            out = out + jnp.dot(y_t, w_ref[lo:hi, :], preferred_element_type=F32)
        for h in range(MEM_HEADS):
            lo, hi = h * dm, (h + 1) * dm
            s = lax.dot_general(qm_ref[:, lo:hi], mk_ref[:, lo:hi], _NT,
                                preferred_element_type=F32) * scale
            p = jnp.exp(s - jnp.max(s, axis=-1, keepdims=True))
            l = jnp.sum(p, axis=-1, keepdims=True)
            o = jnp.dot(p.astype(BF16), mv_ref[:, lo:hi], preferred_element_type=F32) / l
            y_scr[:, lo:hi] = (o * _silu(z_ref[:, MIX_WIDTH + lo:MIX_WIDTH + hi])).astype(BF16)
        if first_step:
            fetch(n_slices - 1)
        out = out + jnp.dot(y_scr[...], w_ref[MIX_WIDTH:, :], preferred_element_type=F32)
        if final:
            ms = jnp.mean(out * out, axis=-1, keepdims=True)
            out = (out * lax.rsqrt(ms + EPS)) * fn_ref[...]
        o_ref[...] = out

    @pl.when(pl.program_id(0) == 0)
    def _():
        body(True)

    @pl.when(pl.program_id(0) != 0)
    def _():
        body(False)


def _out_proj(mix, ob, qm_blk, z, x2d, mkv, w_out, final_norm, batch, seq, tm, final):
    m, d = x2d.shape
    per_b = seq // tm
    row = lambda i: (i, 0)
    return pl.pallas_call(
        functools.partial(_out_kernel, final=final),
        grid=(m // tm,),
        in_specs=[
            pl.BlockSpec((tm, MIX_WIDTH), row),
            pl.BlockSpec((tm, MEM_WIDTH), lambda i: (i, qm_blk)),
            pl.BlockSpec((tm, BRANCH_WIDTH), row),
            pl.BlockSpec((tm, d), row),
            pl.BlockSpec((N_MEM, MEM_WIDTH), lambda i: (i // per_b, 0)),
            pl.BlockSpec((N_MEM, MEM_WIDTH), lambda i: (i // per_b, 1)),
            pl.BlockSpec(memory_space=pl.ANY),
            pl.BlockSpec((1, d), lambda i: (0, 0)),
        ],
        out_specs=pl.BlockSpec((tm, d), row),
        out_shape=jax.ShapeDtypeStruct((m, d), F32),
        scratch_shapes=[pltpu.VMEM((tm, MEM_WIDTH), BF16),
                        pltpu.VMEM((BRANCH_WIDTH, d), BF16),
                        pltpu.VMEM((2, COL_TILE // 2, d), F32),
                        pltpu.SemaphoreType.DMA((2,))],
        compiler_params=_cparams(1),
        name="out_proj",
    )(mix, ob, z, x2d, mkv, mkv, w_out, final_norm.reshape(1, d))


def _in_proj_plan(mixer):
    n_qm = MEM_WIDTH // LANES
    n_z = BRANCH_WIDTH // LANES
    if mixer == 0:
        n_q = SWA_Q_HEADS * SWA_HEAD_DIM // LANES
        plan = [("RST", gq) for gq in range(n_q)]
        plan += [("R", n_qm), ("Rh", n_qm + 1), ("P", n_qm + 2)]
        plan += [("P", j) for j in range(n_qm)]
    elif mixer == 1:
        n_h = MIX_WIDTH // LANES
        plan = [("RST", j) for j in range(n_h)] + [("R", j) for j in range(n_h)]
        plan += [("PT", n_h + j) for j in range(n_h)] + [("P", n_h + j) for j in range(n_qm)]
    else:
        n_qk = RET_HEADS * RET_QK_DIM // LANES
        kinds = ["R"] * n_qk + ["RS"] * n_qk + ["P"] * (MIX_WIDTH // LANES + n_qm)
        plan = [(kind, j) for j, kind in enumerate(kinds)]
    return plan + [("P", None)] * n_z


def kernel(x, mem, positions, mem_norm, w_mem_kv, norm_0, w_in_0, sinks_0, w_out_0, norm_1, w_in_1,
           w_out_1, norm_2, w_in_2, w_out_2, norm_3, w_in_3, sinks_3, w_out_3, final_norm):
    batch, seq, d = x.shape
    m = batch * seq
    layers = [(norm_0, w_in_0, w_out_0, sinks_0), (norm_1, w_in_1, w_out_1, None),
              (norm_2, w_in_2, w_out_2, None), (norm_3, w_in_3, w_out_3, sinks_3)]

    mkv = _mem_kv(mem.reshape(batch * N_MEM, d), mem_norm, w_mem_kv.astype(BF16))

    pos_col = positions.astype(F32).reshape(m, 1)
    rope = (
        (_rope_lane_rows(SWA_HEAD_DIM, SWA_HEAD_DIM // ROPE_FRACTION, ROPE_THETA),
         SWA_HEAD_DIM // ROPE_FRACTION // 2),
        (_rope_lane_rows(MOBA_HEAD_DIM, MOBA_HEAD_DIM // ROPE_FRACTION, ROPE_THETA),
         MOBA_HEAD_DIM // ROPE_FRACTION // 2),
        (_rope_lane_rows(RET_QK_DIM, RET_QK_DIM, RET_THETA), RET_QK_DIM // 2),
    )

    tiles = _tile_sizes(m, seq)
    h = x.reshape(m, d)
    n_layers = len(layers)
    for li, (g, w_in, w_out, sinks) in enumerate(layers):
        mixer = li % N_MIXERS
        lane_rows, half = rope[mixer]
        rs_scale = (SWA_Q_SCALE, MOBA_Q_SCALE, RET_QK_DIM ** -0.5)[mixer]
        ob, z, *qt = _in_proj(h, g, w_in, pos_col, lane_rows, _in_proj_plan(mixer), half,
                              rs_scale, tiles["in_rows"][mixer])
        if mixer == 0:
            mix = _swa(qt[0], ob, sinks, batch, seq)
            qm_blk = 0
        elif mixer == 1:
            mix = _moba(qt[0], ob, batch, seq, cb=tiles["moba_blocks"], qb=tiles["moba_blocks"])
            qm_blk = MIX_WIDTH // MEM_WIDTH
        else:
            mix = _ret(ob, batch, seq, seg=tiles["ret_rows"])
            qm_blk = (2 * RET_HEADS * RET_QK_DIM + MIX_WIDTH) // MEM_WIDTH
        h = _out_proj(mix, ob, qm_blk, z, h, mkv, w_out, final_norm, batch, seq,
                      tiles["out_rows"], final=(li == n_layers - 1))
    return h.reshape(batch, seq, d)
```

```python
import functools
import math

import jax
import jax.numpy as jnp
from jax import lax
from jax.experimental import pallas as pl
from jax.experimental.pallas import tpu as pltpu

F32 = jnp.float32
BF16 = jnp.bfloat16

D_MODEL = 2048
N_MEM = 256
N_MIXERS = 3
BRANCH_WIDTH = D_MODEL
MEM_HEADS = 4
MEM_HEAD_DIM = 128
MEM_WIDTH = MEM_HEADS * MEM_HEAD_DIM
MIX_WIDTH = BRANCH_WIDTH - MEM_WIDTH

SWA_HEAD_DIM = 64
SWA_Q_HEADS = MIX_WIDTH // SWA_HEAD_DIM
SWA_KV_HEADS = SWA_Q_HEADS // 8
SWA_WINDOW = 128

MOBA_HEAD_DIM = 128
MOBA_HEADS = MIX_WIDTH // MOBA_HEAD_DIM
MOBA_BLOCK = 256
MOBA_TOPK = 3

RET_HEADS = 6
RET_V_DIM = MIX_WIDTH // RET_HEADS
RET_QK_DIM = RET_V_DIM // 2
RET_CHUNK = 128
RET_THETA = 10000.0

ROPE_THETA = 500000.0
ROPE_FRACTION = 4
EPS = 1e-6

LANES = 128
COL_TILE = 512
NEG = -1e30
VMEM_LIMIT = 56 * 1024 * 1024

_NT = (((1,), (1,)), ((), ()))
_TN = (((0,), (0,)), ((), ()))


def _cparams(n_axes):
    return pltpu.CompilerParams(dimension_semantics=("arbitrary",) * n_axes,
                                vmem_limit_bytes=VMEM_LIMIT)


def _tile_sizes(m, seq):
    return {
        "in_rows": (min(512, m), min(256, m), min(512, m)),
        "out_rows": min(512, seq),
        "moba_blocks": min(4, seq // MOBA_BLOCK),
        "ret_rows": min(1024, seq),
    }


def _silu(z):
    return z * (1.0 / (1.0 + jnp.exp(-z)))


def _weight_stream(w_hbm, w_scr, stage, sem, slices, axis):
    def window(ref, sl, lead=()):
        idx = (sl, slice(None)) if axis == 0 else (slice(None), sl)
        return ref.at[lead + idx]

    def copy(i):
        sl = slices[i]
        local = slice(0, sl.stop - sl.start)
        return pltpu.make_async_copy(window(w_hbm, sl), window(stage, local, (i % 2,)),
                                     sem.at[i % 2])

    def start():
        copy(0).start()

    def fetch(i):
        if i + 1 < len(slices):
            copy(i + 1).start()
        copy(i).wait()
        sl = slices[i]
        local = slice(0, sl.stop - sl.start)
        idx = (sl, slice(None)) if axis == 0 else (slice(None), sl)
        loc = (local, slice(None)) if axis == 0 else (slice(None), local)
        w_scr[idx] = stage[(i % 2,) + loc].astype(BF16)

    return start, fetch


def _in_proj_kernel(x_ref, g_ref, w_hbm, pos_ref, lane_ref, ob_ref, oz_ref, *rest, plan, half,
                    rs_scale):
    qt_ref = rest[0] if len(rest) == 5 else None
    h_scr, w_ref, stage, sem = rest[-4:]
    gpt = COL_TILE // LANES
    n_grp = len(plan)
    tiles = [(g0, min(g0 + gpt, n_grp)) for g0 in range(0, n_grp, gpt)]
    tiles.sort(key=lambda t: 0 if all(dest is None for _, dest in plan[t[0]:t[1]]) else 1)
    gpp = stage.shape[2] // LANES
    pieces = [(p0, min(p0 + gpp, g1)) for g0, g1 in tiles for p0 in range(g0, g1, gpp)]
    start, fetch_piece = _weight_stream(
        w_hbm, w_ref, stage, sem, [slice(p0 * LANES, p1 * LANES) for p0, p1 in pieces], axis=1)

    def fetch(ti):
        for pi, (p0, _) in enumerate(pieces):
            if tiles[ti][0] <= p0 < tiles[ti][1]:
                fetch_piece(pi)

    def body(first_step):
        if first_step:
            start()
        x = x_ref[...]
        ms = jnp.mean(x * x, axis=-1, keepdims=True)
        h_scr[...] = ((x * lax.rsqrt(ms + EPS)) * g_ref[...]).astype(BF16)

        tables = {}
        anchor = []

        def rope(a, pat):
            if 0 not in tables:
                ang = pos_ref[...] * lane_ref[0:1, :] + anchor[0] * 0.0
                sn = jnp.sin(ang)
                tables[0] = (jnp.cos(ang), sn * lane_ref[1:2, :], sn * lane_ref[2:3, :])
            if pat not in tables:
                first = lax.broadcasted_iota(jnp.int32, tables[0][0].shape, 1) < LANES // 2
                c0, sp0, sm0 = tables[0]
                tables[pat] = (jnp.where(first, c0, 1.0), jnp.where(first, sp0, 0.0),
                               jnp.where(first, sm0, 0.0))
            c, s_plus, s_minus = tables[pat]
            out = a * c + pltpu.roll(a, half, 1) * s_plus
            if 2 * half != LANES:
                out = out + pltpu.roll(a, LANES - half, 1) * s_minus
            return out

        for ti, (g0, g1) in enumerate(tiles):
            if first_step:
                fetch(ti)
            acc = jnp.dot(h_scr[...], w_ref[:, g0 * LANES:g1 * LANES],
                          preferred_element_type=F32)
            if not anchor:
                anchor.append(acc[:, :LANES])
            for gi in range(g0, g1):
                kind, dest = plan[gi]
                a = acc[:, (gi - g0) * LANES:(gi - g0 + 1) * LANES]
                if dest is None:
                    zc = (gi - (n_grp - oz_ref.shape[1] // LANES)) * LANES
                    oz_ref[:, zc:zc + LANES] = a
                    continue
                if kind in ("R", "RS", "RST", "Rh"):
                    a = rope(a, 1 if kind == "Rh" else 0)
                if kind in ("RS", "RST"):
                    a = a * rs_scale
                if kind in ("RST", "PT"):
                    qt_ref[dest * LANES:(dest + 1) * LANES, :] = a.T.astype(BF16)
                else:
                    ob_ref[:, dest * LANES:(dest + 1) * LANES] = a.astype(BF16)

    @pl.when(pl.program_id(0) == 0)
    def _():
        body(True)

    @pl.when(pl.program_id(0) != 0)
    def _():
        body(False)


def _rope_lane_rows(head_dim, rot_dim, theta):
    half = rot_dim // 2
    inv = theta ** (-jnp.arange(0, rot_dim, 2, dtype=F32) / rot_dim)
    rest = head_dim - rot_dim
    z_half, z_rest = jnp.zeros((half,), F32), jnp.zeros((rest,), F32)
    ones = jnp.ones((half,), F32)
    inv_h = jnp.concatenate([inv, inv, z_rest])
    plus_h = jnp.concatenate([z_half, ones, z_rest])
    minus_h = jnp.concatenate([-ones, z_half, z_rest])
    if 2 * half == LANES:
        plus_h, minus_h = plus_h + minus_h, jnp.zeros_like(minus_h)
    rep = LANES // head_dim
    return jnp.stack([jnp.tile(r, rep) for r in (inv_h, plus_h, minus_h)])


def _in_proj(x2d, g, w_in, pos_col, lane_rows, plan, half, rs_scale, tm):
    m, d = x2d.shape
    n_tot = w_in.shape[1]
    n_bf = (max(dest for kind, dest in plan if dest is not None and kind[-1] != "T") + 1) * LANES
    n_qt = sum(kind[-1] == "T" for kind, _ in plan) * LANES
    row = lambda i: (i, 0)
    fixed = lambda i: (0, 0)
    out_specs = [pl.BlockSpec((tm, n_bf), row), pl.BlockSpec((tm, BRANCH_WIDTH), row)]
    out_shape = [jax.ShapeDtypeStruct((m, n_bf), BF16), jax.ShapeDtypeStruct((m, BRANCH_WIDTH), F32)]
    if n_qt:
        out_specs.append(pl.BlockSpec((n_qt, tm), lambda i: (0, i)))
        out_shape.append(jax.ShapeDtypeStruct((n_qt, m), BF16))
    return pl.pallas_call(
        functools.partial(_in_proj_kernel, plan=tuple(plan), half=half, rs_scale=rs_scale),
        grid=(m // tm,),
        in_specs=[
            pl.BlockSpec((tm, d), row),
            pl.BlockSpec((1, d), fixed),
            pl.BlockSpec(memory_space=pl.ANY),
            pl.BlockSpec((tm, 1), row),
            pl.BlockSpec(lane_rows.shape, fixed),
        ],
        out_specs=out_specs,
        out_shape=out_shape,
        scratch_shapes=[pltpu.VMEM((tm, d), BF16),
                        pltpu.VMEM((d, n_tot), BF16),
                        pltpu.VMEM((2, d, COL_TILE // 2), F32),
                        pltpu.SemaphoreType.DMA((2,))],
        compiler_params=_cparams(1),
        name="in_proj",
    )(x2d, g.reshape(1, d), w_in, pos_col, lane_rows)


def _mem_kv_kernel(x_ref, g_ref, w_ref, o_ref):
    x = x_ref[...]
    ms = jnp.mean(x * x, axis=-1, keepdims=True)
    h = ((x * lax.rsqrt(ms + EPS)) * g_ref[...]).astype(BF16)
    o_ref[...] = jnp.dot(h, w_ref[...], preferred_element_type=F32).astype(BF16)


def _mem_kv(mem2d, g, w_bf16):
    m, d = mem2d.shape
    n = w_bf16.shape[1]
    tm = min(m, 256)
    return pl.pallas_call(
        _mem_kv_kernel,
        grid=(m // tm,),
        in_specs=[
            pl.BlockSpec((tm, d), lambda i: (i, 0)),
            pl.BlockSpec((1, d), lambda i: (0, 0)),
            pl.BlockSpec((d, n), lambda i: (0, 0)),
        ],
        out_specs=pl.BlockSpec((tm, n), lambda i: (i, 0)),
        out_shape=jax.ShapeDtypeStruct((m, n), BF16),
        compiler_params=_cparams(1),
        name="mem_kv",
    )(mem2d, g.reshape(1, d), w_bf16)


SWA_Q_SCALE = SWA_HEAD_DIM ** -0.5 * math.log2(math.e)


SWA_SUB = 8


def _swa_kernel(sink_ref, qt_ref, ka_ref, kb_ref, kc_ref, o_ref, vt, s_scr, *, seq):
    w = SWA_WINDOW
    dh = SWA_HEAD_DIM
    g_per = SWA_Q_HEADS // SWA_KV_HEADS
    step = pl.program_id(1)

    @pl.when(step == 0)
    def _():
        for j in range(seq // w):
            rows = slice(j * w, (j + 1) * w)
            both = jnp.concatenate([kb_ref[rows, :], kc_ref[rows, :]], axis=1)
            vt[j] = both.astype(F32).T.astype(BF16)

    zeros = jnp.zeros((dh, g_per * w), BF16)
    vt_wins = []
    for sub in range(SWA_SUB):
        t = step * SWA_SUB + sub
        j0 = jnp.maximum(t - 1, 0)
        r0 = pl.multiple_of(j0 * w, w)
        k_wins = (ka_ref[pl.ds(r0, 2 * w), :], kb_ref[pl.ds(r0, 2 * w), :])
        vt_wins.append(jnp.concatenate([vt[j0], vt[j0 + 1]], axis=1))
        kpos = r0 + lax.broadcasted_iota(jnp.int32, (2 * w, w), 0)
        qpos = t * w + lax.broadcasted_iota(jnp.int32, (2 * w, w), 1)
        bias = jnp.where((kpos <= qpos) & (kpos > qpos - w), 0.0, NEG)
        bias = jnp.concatenate([bias] * g_per, axis=1)
        for h in range(SWA_KV_HEADS):
            q_grp = jnp.concatenate(
                [qt_ref[(h * g_per + g) * dh:(h * g_per + g + 1) * dh, sub * w:(sub + 1) * w]
                 for g in range(g_per)], axis=1)
            q_pad = jnp.concatenate([q_grp, zeros] if h % 2 == 0 else [zeros, q_grp], axis=0)
            s_scr[sub * SWA_KV_HEADS + h] = (
                jnp.dot(k_wins[h // 2], q_pad, preferred_element_type=F32) + bias)
    for sub in range(SWA_SUB):
        for h in range(SWA_KV_HEADS):
            s = s_scr[sub * SWA_KV_HEADS + h]
            sink = sink_ref[h:h + 1, :]
            mx = jnp.maximum(jnp.max(s, axis=0, keepdims=True), sink)
            p = jnp.exp2(s - mx)
            denom = jnp.sum(p, axis=0, keepdims=True) + jnp.exp2(sink - mx)
            o_t = jnp.dot(vt_wins[sub][(h + 1) * dh:(h + 2) * dh, :], p.astype(BF16),
                          preferred_element_type=F32) / denom
            for pair in range(g_per // 2):
                two = jnp.concatenate([o_t[:, (2 * pair) * w:(2 * pair + 1) * w],
                                       o_t[:, (2 * pair + 1) * w:(2 * pair + 2) * w]], axis=0)
                c0 = (h * g_per + 2 * pair) * dh
                o_ref[sub * w:(sub + 1) * w, c0:c0 + 2 * dh] = two.T.astype(BF16)


def _swa(qt, ob, sinks, batch, seq):
    m = batch * seq
    w = SWA_WINDOW
    nq = SWA_Q_HEADS * SWA_HEAD_DIM
    g_per = SWA_Q_HEADS // SWA_KV_HEADS
    n_t = seq // (SWA_SUB * w)
    kv_grp = MEM_WIDTH // LANES
    sink_rows = jnp.repeat(sinks.astype(F32) * math.log2(math.e), w).reshape(SWA_KV_HEADS, g_per * w)
    kv_spec = lambda j: pl.BlockSpec((seq, LANES), lambda b, t: (b, kv_grp + j))
    return pl.pallas_call(
        functools.partial(_swa_kernel, seq=seq),
        grid=(batch, n_t),
        in_specs=[
            pl.BlockSpec((SWA_KV_HEADS, g_per * w), lambda b, t: (0, 0)),
            pl.BlockSpec((nq, SWA_SUB * w), lambda b, t: (0, b * n_t + t)),
            kv_spec(0), kv_spec(1), kv_spec(2),
        ],
        out_specs=pl.BlockSpec((SWA_SUB * w, nq), lambda b, t: (b * n_t + t, 0)),
        out_shape=jax.ShapeDtypeStruct((m, nq), BF16),
        scratch_shapes=[pltpu.VMEM((seq // w, 2 * LANES, w), BF16),
                        pltpu.VMEM((SWA_SUB * SWA_KV_HEADS, 2 * w, g_per * w), F32)],
        compiler_params=_cparams(2),
        name="swa",
    )(sink_rows, qt, ob, ob, ob)


MOBA_BIAS_ROWS = 16
MOBA_Q_SCALE = MOBA_HEAD_DIM ** -0.5 * math.log2(math.e)


def _moba_kernel(qt_ref, k_ref, vt, o_ref, kaug, kmean, qaug, s_a, s_b, *, seq, cb, qb):
    blk = MOBA_BLOCK
    dh = MOBA_HEAD_DIM
    nblk = seq // blk
    kc = cb * blk
    wq = qb * blk
    nbr = MOBA_BIAS_ROWS
    i = pl.program_id(2)

    first_call_step = (pl.program_id(0) == 0) & (pl.program_id(1) == 0) & (i == 0)

    @pl.when(first_call_step)
    def _():
        row_blk = lax.broadcasted_iota(jnp.int32, (seq, LANES), 0) // blk
        lane = lax.broadcasted_iota(jnp.int32, (seq, LANES), 1)
        kaug[:, dh:] = (row_blk == lane).astype(BF16)
        qaug[dh + nbr:, :] = jnp.zeros((dh - nbr, wq), BF16)
        kmean[...] = jnp.zeros_like(kmean)

    @pl.when(i == 0)
    def _():
        k = k_ref[...]
        kaug[:, :dh] = k
        kmean[:nblk, :] = jnp.mean(k.astype(F32).reshape(nblk, blk, dh), axis=1)

    q_t = qt_ref[...]
    gate = jnp.dot(kmean[...].astype(BF16), q_t, preferred_element_type=F32)
    row = lax.broadcasted_iota(jnp.int32, (nbr, wq), 0)
    rowf = row.astype(F32)
    own = i * qb + lax.broadcasted_iota(jnp.int32, (nbr, wq), 1) // blk
    past = row < own
    g = jnp.where(past, gate, -jnp.inf)
    sel = row == own
    for _ in range(min(MOBA_TOPK, nblk - 1)):
        mx = jnp.max(g, axis=0, keepdims=True)
        idx = jnp.min(jnp.where(g == mx, rowf, float(nbr)), axis=0, keepdims=True)
        hit = rowf == idx
        sel = sel | (hit & past)
        g = jnp.where(hit, -jnp.inf, g)
    qaug[:dh, :] = q_t
    qaug[dh:dh + nbr, :] = jnp.where(sel, 0.0, NEG).astype(BF16)

    causal = (lax.broadcasted_iota(jnp.int32, (blk, blk), 0)
              <= lax.broadcasted_iota(jnp.int32, (blk, blk), 1))
    for case in range(nblk // cb):
        @pl.when(i == case)
        def _(case=case):
            order = [case] + list(range(case))
            s_bufs = (s_a, s_b)

            def stage_scores(idx):
                c = order[idx]
                s = jnp.dot(kaug[c * kc:(c + 1) * kc, :], qaug[...],
                            preferred_element_type=F32)
                s_bufs[idx % 2][...] = s
                if c == case:
                    for d in range(qb):
                        diag = slice(d * blk, (d + 1) * blk)
                        s_bufs[idx % 2][diag, diag] = jnp.where(causal, s[diag, diag], NEG)

            stage_scores(0)
            m_col = l_col = acc = None
            for idx, c in enumerate(order):
                if idx + 1 < len(order):
                    stage_scores(idx + 1)
                rows = slice(c * kc, (c + 1) * kc)
                s = s_bufs[idx % 2][...]
                m_c = jnp.max(s, axis=0, keepdims=True)
                if m_col is None:
                    m_col = m_c
                    p = jnp.exp2(s - m_col)
                    l_col = jnp.sum(p, axis=0, keepdims=True)
                    acc = jnp.dot(vt[:, rows], p.astype(BF16), preferred_element_type=F32)
                else:
                    m_new = jnp.maximum(m_col, m_c)
                    alpha = jnp.exp2(m_col - m_new)
                    p = jnp.exp2(s - m_new)
                    l_col = alpha * l_col + jnp.sum(p, axis=0, keepdims=True)
                    acc = alpha * acc + jnp.dot(vt[:, rows], p.astype(BF16),
                                                preferred_element_type=F32)
                    m_col = m_new
            o_ref[...] = (acc / l_col).T.astype(BF16)


def _moba(qvt, ob, batch, seq, cb, qb):
    assert cb == qb
    m = batch * seq
    blk = MOBA_BLOCK
    dh = MOBA_HEAD_DIM
    nh = MOBA_HEADS
    n_q = seq // (qb * blk)
    wq = qb * blk
    return pl.pallas_call(
        functools.partial(_moba_kernel, seq=seq, cb=cb, qb=qb),
        grid=(batch, nh, n_q),
        in_specs=[
            pl.BlockSpec((dh, wq), lambda b, h, i: (h, b * n_q + i)),
            pl.BlockSpec((seq, dh), lambda b, h, i: (b, h)),
            pl.BlockSpec((dh, seq), lambda b, h, i: (nh + h, b)),
        ],
        out_specs=pl.BlockSpec((wq, dh), lambda b, h, i: (b * n_q + i, h)),
        out_shape=jax.ShapeDtypeStruct((m, nh * dh), BF16),
        scratch_shapes=[pltpu.VMEM((seq, 2 * dh), BF16),
                        pltpu.VMEM((MOBA_BIAS_ROWS, dh), F32),
                        pltpu.VMEM((2 * dh, wq), BF16),
                        pltpu.VMEM((cb * blk, wq), F32),
                        pltpu.VMEM((cb * blk, wq), F32)],
        compiler_params=_cparams(3),
        name="moba",
    )(qvt, ob, qvt)


def _ret_kernel(qk_ref, v_ref, dec_ref, xi_ref, zeta_ref, o_ref, r_scr, *, n_chunk, g_chunk):
    t = RET_CHUNK
    nh, dk, dv = RET_HEADS, RET_QK_DIM, RET_V_DIM

    @pl.when(pl.program_id(1) == 0)
    def _():
        r_scr[...] = jnp.zeros_like(r_scr)

    def body(c, carry):
        r0 = pl.multiple_of(c * t, t)
        for h in range(nh):
            q = qk_ref[pl.ds(r0, t), h * dk:(h + 1) * dk]
            k = qk_ref[pl.ds(r0, t), (nh + h) * dk:(nh + h + 1) * dk]
            v = v_ref[pl.ds(r0, t), h * dv:(h + 1) * dv]
            s = lax.dot_general(q, k, _NT, preferred_element_type=F32) * dec_ref[h]
            inner = jnp.dot(s.astype(BF16), v, preferred_element_type=F32)
            r_prev = r_scr[h]
            q_x = (q.astype(F32) * xi_ref[h]).astype(BF16)
            cross = jnp.dot(q_x, r_prev.astype(BF16), preferred_element_type=F32)
            o = inner + cross
            o = o * lax.rsqrt(jnp.mean(o * o, axis=-1, keepdims=True) + EPS)
            o_ref[pl.ds(r0, t), h * dv:(h + 1) * dv] = o.astype(BF16)
            k_z = (k.astype(F32) * zeta_ref[h]).astype(BF16)
            kv = lax.dot_general(k_z, v, _TN, preferred_element_type=F32)
            r_scr[h] = g_chunk[h] * r_prev + kv
        return carry

    lax.fori_loop(0, n_chunk, body, 0, unroll=True)


def _ret_constants():
    nh, t, dk = RET_HEADS, RET_CHUNK, RET_QK_DIM
    lin = [math.log(1.0 / 32) + (math.log(1.0 / 512) - math.log(1.0 / 32)) * h / (nh - 1)
           for h in range(nh)]
    log_g = [math.log1p(-math.exp(v)) for v in lin]
    i = jnp.arange(t, dtype=F32)
    lg = jnp.asarray(log_g, F32)
    diff = i[:, None] - i[None, :]
    decay = jnp.where(diff >= 0, jnp.exp(jnp.maximum(diff, 0.0)[None] * lg[:, None, None]), 0.0)
    xi = jnp.exp((i + 1)[None, :] * lg[:, None])
    zeta = jnp.exp((t - 1 - i)[None, :] * lg[:, None])
    xi_t = jnp.broadcast_to(xi[:, :, None], (nh, t, dk))
    zeta_t = jnp.broadcast_to(zeta[:, :, None], (nh, t, dk))
    g_chunk = tuple(math.exp(t * v) for v in log_g)
    return decay, xi_t, zeta_t, g_chunk


def _ret(ob, batch, seq, seg):
    m = batch * seq
    nh, dk, dv, t = RET_HEADS, RET_QK_DIM, RET_V_DIM, RET_CHUNK
    n_seg = seq // seg
    decay, xi_t, zeta_t, g_chunk = _ret_constants()
    wqk = 2 * nh * dk
    wv = nh * dv
    const = lambda b, s: (0, 0, 0)
    return pl.pallas_call(
        functools.partial(_ret_kernel, n_chunk=seg // t, g_chunk=g_chunk),
        grid=(batch, n_seg),
        in_specs=[
            pl.BlockSpec((seg, wqk), lambda b, s: (b * n_seg + s, 0)),
            pl.BlockSpec((seg, wv), lambda b, s: (b * n_seg + s, wqk // wv)),
            pl.BlockSpec((nh, t, t), const),
            pl.BlockSpec((nh, t, dk), const),
            pl.BlockSpec((nh, t, dk), const),
        ],
        out_specs=pl.BlockSpec((seg, wv), lambda b, s: (b * n_seg + s, 0)),
        out_shape=jax.ShapeDtypeStruct((m, wv), BF16),
        scratch_shapes=[pltpu.VMEM((nh, dk, dv), F32)],
        compiler_params=_cparams(2),
        name="retention",
    )(ob, ob, decay, xi_t, zeta_t)


def _out_kernel(mix_ref, qm_ref, z_ref, x_ref, mk_ref, mv_ref, w_hbm, fn_ref, o_ref, y_scr, w_ref,
                stage, sem, *, final):
    dm = MEM_HEAD_DIM
    scale = dm ** -0.5
    n_slices = BRANCH_WIDTH // COL_TILE
    rows = stage.shape[1]
    per = COL_TILE // rows
    start, fetch_piece = _weight_stream(
        w_hbm, w_ref, stage, sem,
        [slice(j * rows, (j + 1) * rows) for j in range(BRANCH_WIDTH // rows)], axis=0)

    def fetch(t):
        for j in range(per):
            fetch_piece(t * per + j)

    def body(first_step):
        if first_step:
            start()
        out = x_ref[...]
        for t in range(MIX_WIDTH // COL_TILE):
            lo, hi = t * COL_TILE, (t + 1) * COL_TILE
            y_t = (mix_ref[:, lo:hi].astype(F32) * _silu(z_ref[:, lo:hi])).astype(BF16)
            if first_step:
                fetch(t)
            out = out + jnp.dot(y_t, w_ref[lo:hi, :], preferred_element_type=F32)
        for h in range(MEM_HEADS):
            lo, hi = h * dm, (h + 1) * dm
            s = lax.dot_general(qm_ref[:, lo:hi], mk_ref[:, lo:hi], _NT,
                                preferred_element_type=F32) * scale
            p = jnp.exp(s - jnp.max(s, axis=-1, keepdims=True))
            l = jnp.sum(p, axis=-1, keepdims=True)
            o = jnp.dot(p.astype(BF16), mv_ref[:, lo:hi], preferred_element_type=F32) / l
            y_scr[:, lo:hi] = (o * _silu(z_ref[:, MIX_WIDTH + lo:MIX_WIDTH + hi])).astype(BF16)
        if first_step:
            fetch(n_slices - 1)
        out = out + jnp.dot(y_scr[...], w_ref[MIX_WIDTH:, :], preferred_element_type=F32)
        if final:
            ms = jnp.mean(out * out, axis=-1, keepdims=True)
            out = (out * lax.rsqrt(ms + EPS)) * fn_ref[...]
        o_ref[...] = out

    @pl.when(pl.program_id(0) == 0)
    def _():
        body(True)

    @pl.when(pl.program_id(0) != 0)
    def _():
        body(False)


def _out_proj(mix, ob, qm_blk, z, x2d, mkv, w_out, final_norm, batch, seq, tm, final):
    m, d = x2d.shape
    per_b = seq // tm
    row = lambda i: (i, 0)
    return pl.pallas_call(
        functools.partial(_out_kernel, final=final),
        grid=(m // tm,),
        in_specs=[
            pl.BlockSpec((tm, MIX_WIDTH), row),
            pl.BlockSpec((tm, MEM_WIDTH), lambda i: (i, qm_blk)),
            pl.BlockSpec((tm, BRANCH_WIDTH), row),
            pl.BlockSpec((tm, d), row),
            pl.BlockSpec((N_MEM, MEM_WIDTH), lambda i: (i // per_b, 0)),
            pl.BlockSpec((N_MEM, MEM_WIDTH), lambda i: (i // per_b, 1)),
            pl.BlockSpec(memory_space=pl.ANY),
            pl.BlockSpec((1, d), lambda i: (0, 0)),
        ],
        out_specs=pl.BlockSpec((tm, d), row),
        out_shape=jax.ShapeDtypeStruct((m, d), F32),
        scratch_shapes=[pltpu.VMEM((tm, MEM_WIDTH), BF16),
                        pltpu.VMEM((BRANCH_WIDTH, d), BF16),
                        pltpu.VMEM((2, COL_TILE // 2, d), F32),
                        pltpu.SemaphoreType.DMA((2,))],
        compiler_params=_cparams(1),
        name="out_proj",
    )(mix, ob, z, x2d, mkv, mkv, w_out, final_norm.reshape(1, d))


def _in_proj_plan(mixer):
    n_qm = MEM_WIDTH // LANES
    n_z = BRANCH_WIDTH // LANES
    if mixer == 0:
        n_q = SWA_Q_HEADS * SWA_HEAD_DIM // LANES
        plan = [("RST", gq) for gq in range(n_q)]
        plan += [("R", n_qm), ("Rh", n_qm + 1), ("P", n_qm + 2)]
        plan += [("P", j) for j in range(n_qm)]
    elif mixer == 1:
        n_h = MIX_WIDTH // LANES
        plan = [("RST", j) for j in range(n_h)] + [("R", j) for j in range(n_h)]
        plan += [("PT", n_h + j) for j in range(n_h)] + [("P", n_h + j) for j in range(n_qm)]
    else:
        n_qk = RET_HEADS * RET_QK_DIM // LANES
        kinds = ["R"] * n_qk + ["RS"] * n_qk + ["P"] * (MIX_WIDTH // LANES + n_qm)
        plan = [(kind, j) for j, kind in enumerate(kinds)]
    return plan + [("P", None)] * n_z


def kernel(x, mem, positions, mem_norm, w_mem_kv, norm_0, w_in_0, sinks_0, w_out_0, norm_1, w_in_1,
           w_out_1, norm_2, w_in_2, w_out_2, norm_3, w_in_3, sinks_3, w_out_3, final_norm):
    batch, seq, d = x.shape
    m = batch * seq
    layers = [(norm_0, w_in_0, w_out_0, sinks_0), (norm_1, w_in_1, w_out_1, None),
              (norm_2, w_in_2, w_out_2, None), (norm_3, w_in_3, w_out_3, sinks_3)]

    mkv = _mem_kv(mem.reshape(batch * N_MEM, d), mem_norm, w_mem_kv.astype(BF16))

    pos_col = positions.astype(F32).reshape(m, 1)
    rope = (
        (_rope_lane_rows(SWA_HEAD_DIM, SWA_HEAD_DIM // ROPE_FRACTION, ROPE_THETA),
         SWA_HEAD_DIM // ROPE_FRACTION // 2),
        (_rope_lane_rows(MOBA_HEAD_DIM, MOBA_HEAD_DIM // ROPE_FRACTION, ROPE_THETA),
         MOBA_HEAD_DIM // ROPE_FRACTION // 2),
        (_rope_lane_rows(RET_QK_DIM, RET_QK_DIM, RET_THETA), RET_QK_DIM // 2),
    )

    tiles = _tile_sizes(m, seq)
    h = x.reshape(m, d)
    n_layers = len(layers)
    for li, (g, w_in, w_out, sinks) in enumerate(layers):
        mixer = li % N_MIXERS
        lane_rows, half = rope[mixer]
        rs_scale = (SWA_Q_SCALE, MOBA_Q_SCALE, RET_QK_DIM ** -0.5)[mixer]
        ob, z, *qt = _in_proj(h, g, w_in, pos_col, lane_rows, _in_proj_plan(mixer), half,
                              rs_scale, tiles["in_rows"][mixer])
        if mixer == 0:
            mix = _swa(qt[0], ob, sinks, batch, seq)
            qm_blk = 0
        elif mixer == 1:
            mix = _moba(qt[0], ob, batch, seq, cb=tiles["moba_blocks"], qb=tiles["moba_blocks"])
            qm_blk = MIX_WIDTH // MEM_WIDTH
        else:
            mix = _ret(ob, batch, seq, seg=tiles["ret_rows"])
            qm_blk = (2 * RET_HEADS * RET_QK_DIM + MIX_WIDTH) // MEM_WIDTH
        h = _out_proj(mix, ob, qm_blk, z, h, mkv, w_out, final_norm, batch, seq,
                      tiles["out_rows"], final=(li == n_layers - 1))
    return h.reshape(batch, seq, d)
```

```python
import functools
import math

import jax
import jax.numpy as jnp
from jax import lax
from jax.experimental import pallas as pl
from jax.experimental.pallas import tpu as pltpu

F32 = jnp.float32
BF16 = jnp.bfloat16

D_MODEL = 2048
N_MEM = 256
N_MIXERS = 3
BRANCH_WIDTH = D_MODEL
MEM_HEADS = 4
MEM_HEAD_DIM = 128
MEM_WIDTH = MEM_HEADS * MEM_HEAD_DIM
MIX_WIDTH = BRANCH_WIDTH - MEM_WIDTH

SWA_HEAD_DIM = 64
SWA_Q_HEADS = MIX_WIDTH // SWA_HEAD_DIM
SWA_KV_HEADS = SWA_Q_HEADS // 8
SWA_WINDOW = 128

MOBA_HEAD_DIM = 128
MOBA_HEADS = MIX_WIDTH // MOBA_HEAD_DIM
MOBA_BLOCK = 256
MOBA_TOPK = 3

RET_HEADS = 6
RET_V_DIM = MIX_WIDTH // RET_HEADS
RET_QK_DIM = RET_V_DIM // 2
RET_CHUNK = 128
RET_THETA = 10000.0

ROPE_THETA = 500000.0
ROPE_FRACTION = 4
EPS = 1e-6

LANES = 128
COL_TILE = 512
NEG = -1e30
VMEM_LIMIT = 56 * 1024 * 1024

_NT = (((1,), (1,)), ((), ()))
_TN = (((0,), (0,)), ((), ()))


def _cparams(n_axes):
    return pltpu.CompilerParams(dimension_semantics=("arbitrary",) * n_axes,
                                vmem_limit_bytes=VMEM_LIMIT)


def _tile_sizes(m, seq):
    return {
        "in_rows": (min(512, m), min(256, m), min(512, m)),
        "out_rows": min(512, seq),
        "moba_blocks": min(4, seq // MOBA_BLOCK),
        "ret_rows": min(1024, seq),
    }


def _silu(z):
    return z * (1.0 / (1.0 + jnp.exp(-z)))


def _weight_stream(w_hbm, w_scr, stage, sem, slices, axis):
    def window(ref, sl, lead=()):
        idx = (sl, slice(None)) if axis == 0 else (slice(None), sl)
        return ref.at[lead + idx]

    def copy(i):
        sl = slices[i]
        local = slice(0, sl.stop - sl.start)
        return pltpu.make_async_copy(window(w_hbm, sl), window(stage, local, (i % 2,)),
                                     sem.at[i % 2])

    def start():
        copy(0).start()

    def fetch(i):
        if i + 1 < len(slices):
            copy(i + 1).start()
        copy(i).wait()
        sl = slices[i]
        local = slice(0, sl.stop - sl.start)
        idx = (sl, slice(None)) if axis == 0 else (slice(None), sl)
        loc = (local, slice(None)) if axis == 0 else (slice(None), local)
        w_scr[idx] = stage[(i % 2,) + loc].astype(BF16)

    return start, fetch


def _in_proj_kernel(x_ref, g_ref, w_hbm, pos_ref, lane_ref, ob_ref, oz_ref, *rest, plan, half,
                    rs_scale):
    qt_ref = rest[0] if len(rest) == 5 else None
    h_scr, w_ref, stage, sem = rest[-4:]
    gpt = COL_TILE // LANES
    n_grp = len(plan)
    tiles = [(g0, min(g0 + gpt, n_grp)) for g0 in range(0, n_grp, gpt)]
    tiles.sort(key=lambda t: 0 if all(dest is None for _, dest in plan[t[0]:t[1]]) else 1)
    gpp = stage.shape[2] // LANES
    pieces = [(p0, min(p0 + gpp, g1)) for g0, g1 in tiles for p0 in range(g0, g1, gpp)]
    start, fetch_piece = _weight_stream(
        w_hbm, w_ref, stage, sem, [slice(p0 * LANES, p1 * LANES) for p0, p1 in pieces], axis=1)

    def fetch(ti):
        for pi, (p0, _) in enumerate(pieces):
            if tiles[ti][0] <= p0 < tiles[ti][1]:
                fetch_piece(pi)

    def body(first_step):
        if first_step:
            start()
        x = x_ref[...]
        ms = jnp.mean(x * x, axis=-1, keepdims=True)
        h_scr[...] = ((x * lax.rsqrt(ms + EPS)) * g_ref[...]).astype(BF16)

        tables = {}
        anchor = []

        def rope(a, pat):
            if 0 not in tables:
                ang = pos_ref[...] * lane_ref[0:1, :] + anchor[0] * 0.0
                sn = jnp.sin(ang)
                tables[0] = (jnp.cos(ang), sn * lane_ref[1:2, :], sn * lane_ref[2:3, :])
            if pat not in tables:
                first = lax.broadcasted_iota(jnp.int32, tables[0][0].shape, 1) < LANES // 2
                c0, sp0, sm0 = tables[0]
                tables[pat] = (jnp.where(first, c0, 1.0), jnp.where(first, sp0, 0.0),
                               jnp.where(first, sm0, 0.0))
            c, s_plus, s_minus = tables[pat]
            out = a * c + pltpu.roll(a, half, 1) * s_plus
            if 2 * half != LANES:
                out = out + pltpu.roll(a, LANES - half, 1) * s_minus
            return out

        for ti, (g0, g1) in enumerate(tiles):
            if first_step:
                fetch(ti)
            acc = jnp.dot(h_scr[...], w_ref[:, g0 * LANES:g1 * LANES],
                          preferred_element_type=F32)
            if not anchor:
                anchor.append(acc[:, :LANES])
            for gi in range(g0, g1):
                kind, dest = plan[gi]
                a = acc[:, (gi - g0) * LANES:(gi - g0 + 1) * LANES]
                if dest is None:
                    zc = (gi - (n_grp - oz_ref.shape[1] // LANES)) * LANES
                    oz_ref[:, zc:zc + LANES] = a
                    continue
                if kind in ("R", "RS", "RST", "Rh"):
                    a = rope(a, 1 if kind == "Rh" else 0)
                if kind in ("RS", "RST"):
                    a = a * rs_scale
                if kind in ("RST", "PT"):
                    qt_ref[dest * LANES:(dest + 1) * LANES, :] = a.T.astype(BF16)
                else:
                    ob_ref[:, dest * LANES:(dest + 1) * LANES] = a.astype(BF16)

    @pl.when(pl.program_id(0) == 0)
    def _():
        body(True)

    @pl.when(pl.program_id(0) != 0)
    def _():
        body(False)


def _rope_lane_rows(head_dim, rot_dim, theta):
    half = rot_dim // 2
    inv = theta ** (-jnp.arange(0, rot_dim, 2, dtype=F32) / rot_dim)
    rest = head_dim - rot_dim
    z_half, z_rest = jnp.zeros((half,), F32), jnp.zeros((rest,), F32)
    ones = jnp.ones((half,), F32)
    inv_h = jnp.concatenate([inv, inv, z_rest])
    plus_h = jnp.concatenate([z_half, ones, z_rest])
    minus_h = jnp.concatenate([-ones, z_half, z_rest])
    if 2 * half == LANES:
        plus_h, minus_h = plus_h + minus_h, jnp.zeros_like(minus_h)
    rep = LANES // head_dim
    return jnp.stack([jnp.tile(r, rep) for r in (inv_h, plus_h, minus_h)])


def _in_proj(x2d, g, w_in, pos_col, lane_rows, plan, half, rs_scale, tm):
    m, d = x2d.shape
    n_tot = w_in.shape[1]
    n_bf = (max(dest for kind, dest in plan if dest is not None and kind[-1] != "T") + 1) * LANES
    n_qt = sum(kind[-1] == "T" for kind, _ in plan) * LANES
    row = lambda i: (i, 0)
    fixed = lambda i: (0, 0)
    out_specs = [pl.BlockSpec((tm, n_bf), row), pl.BlockSpec((tm, BRANCH_WIDTH), row)]
    out_shape = [jax.ShapeDtypeStruct((m, n_bf), BF16), jax.ShapeDtypeStruct((m, BRANCH_WIDTH), F32)]
    if n_qt:
        out_specs.append(pl.BlockSpec((n_qt, tm), lambda i: (0, i)))
        out_shape.append(jax.ShapeDtypeStruct((n_qt, m), BF16))
    return pl.pallas_call(
        functools.partial(_in_proj_kernel, plan=tuple(plan), half=half, rs_scale=rs_scale),
        grid=(m // tm,),
        in_specs=[
            pl.BlockSpec((tm, d), row),
            pl.BlockSpec((1, d), fixed),
            pl.BlockSpec(memory_space=pl.ANY),
            pl.BlockSpec((tm, 1), row),
            pl.BlockSpec(lane_rows.shape, fixed),
        ],
        out_specs=out_specs,
        out_shape=out_shape,
        scratch_shapes=[pltpu.VMEM((tm, d), BF16),
                        pltpu.VMEM((d, n_tot), BF16),
                        pltpu.VMEM((2, d, COL_TILE // 2), F32),
                        pltpu.SemaphoreType.DMA((2,))],
        compiler_params=_cparams(1),
        name="in_proj",
    )(x2d, g.reshape(1, d), w_in, pos_col, lane_rows)


def _mem_kv_kernel(x_ref, g_ref, w_ref, o_ref):
    x = x_ref[...]
    ms = jnp.mean(x * x, axis=-1, keepdims=True)
    h = ((x * lax.rsqrt(ms + EPS)) * g_ref[...]).astype(BF16)
    o_ref[...] = jnp.dot(h, w_ref[...], preferred_element_type=F32).astype(BF16)


def _mem_kv(mem2d, g, w_bf16):
    m, d = mem2d.shape
    n = w_bf16.shape[1]
    tm = min(m, 256)
    return pl.pallas_call(
        _mem_kv_kernel,
        grid=(m // tm,),
        in_specs=[
            pl.BlockSpec((tm, d), lambda i: (i, 0)),
            pl.BlockSpec((1, d), lambda i: (0, 0)),
            pl.BlockSpec((d, n), lambda i: (0, 0)),
        ],
        out_specs=pl.BlockSpec((tm, n), lambda i: (i, 0)),
        out_shape=jax.ShapeDtypeStruct((m, n), BF16),
        compiler_params=_cparams(1),
        name="mem_kv",
    )(mem2d, g.reshape(1, d), w_bf16)


SWA_Q_SCALE = SWA_HEAD_DIM ** -0.5 * math.log2(math.e)


SWA_SUB = 8


def _swa_kernel(sink_ref, qt_ref, ka_ref, kb_ref, kc_ref, o_ref, vt, s_scr, *, seq):
    w = SWA_WINDOW
    dh = SWA_HEAD_DIM
    g_per = SWA_Q_HEADS // SWA_KV_HEADS
    step = pl.program_id(1)

    @pl.when(step == 0)
    def _():
        for j in range(seq // w):
            rows = slice(j * w, (j + 1) * w)
            both = jnp.concatenate([kb_ref[rows, :], kc_ref[rows, :]], axis=1)
            vt[j] = both.astype(F32).T.astype(BF16)

    zeros = jnp.zeros((dh, g_per * w), BF16)
    vt_wins = []
    upper = (lax.broadcasted_iota(jnp.int32, (w, g_per * w), 0)
             > (lax.broadcasted_iota(jnp.int32, (w, g_per * w), 1) & (w - 1)))
    no_prev = jnp.where(step == 0, NEG, 0.0)
    for sub in range(SWA_SUB):
        t = step * SWA_SUB + sub
        j0 = jnp.maximum(t - 1, 0)
        r_prev = pl.multiple_of(j0 * w, w)
        r_own = pl.multiple_of(t * w, w)
        k_wins = tuple(jnp.concatenate([ref[pl.ds(r_prev, w), :], ref[pl.ds(r_own, w), :]], axis=0)
                       for ref in (ka_ref, kb_ref))
        vt_wins.append(jnp.concatenate([vt[j0], vt[t]], axis=1))
        for h in range(SWA_KV_HEADS):
            q_grp = jnp.concatenate(
                [qt_ref[(h * g_per + g) * dh:(h * g_per + g + 1) * dh, sub * w:(sub + 1) * w]
                 for g in range(g_per)], axis=1)
            q_pad = jnp.concatenate([q_grp, zeros] if h % 2 == 0 else [zeros, q_grp], axis=0)
            s2 = jnp.dot(k_wins[h // 2], q_pad, preferred_element_type=F32)
            s_prev = s2[:w] + no_prev if sub == 0 else s2[:w]
            s_scr[sub * SWA_KV_HEADS + h] = jnp.where(upper, s_prev, s2[w:])
    for sub in range(SWA_SUB):
        for h in range(SWA_KV_HEADS):
            s = s_scr[sub * SWA_KV_HEADS + h]
            sink = sink_ref[h:h + 1, :]
            mx = jnp.maximum(jnp.max(s, axis=0, keepdims=True), sink)
            p = jnp.exp2(s - mx)
            denom = jnp.sum(p, axis=0, keepdims=True) + jnp.exp2(sink - mx)
            p2 = jnp.concatenate([jnp.where(upper, p, 0.0), jnp.where(upper, 0.0, p)], axis=0)
            o_t = jnp.dot(vt_wins[sub][(h + 1) * dh:(h + 2) * dh, :], p2.astype(BF16),
                          preferred_element_type=F32) / denom
            for pair in range(g_per // 2):
                two = jnp.concatenate([o_t[:, (2 * pair) * w:(2 * pair + 1) * w],
                                       o_t[:, (2 * pair + 1) * w:(2 * pair + 2) * w]], axis=0)
                c0 = (h * g_per + 2 * pair) * dh
                o_ref[sub * w:(sub + 1) * w, c0:c0 + 2 * dh] = two.T.astype(BF16)


def _swa(qt, ob, sinks, batch, seq):
    m = batch * seq
    w = SWA_WINDOW
    nq = SWA_Q_HEADS * SWA_HEAD_DIM
    g_per = SWA_Q_HEADS // SWA_KV_HEADS
    n_t = seq // (SWA_SUB * w)
    kv_grp = MEM_WIDTH // LANES
    sink_rows = jnp.repeat(sinks.astype(F32) * math.log2(math.e), w).reshape(SWA_KV_HEADS, g_per * w)
    kv_spec = lambda j: pl.BlockSpec((seq, LANES), lambda b, t: (b, kv_grp + j))
    return pl.pallas_call(
        functools.partial(_swa_kernel, seq=seq),
        grid=(batch, n_t),
        in_specs=[
            pl.BlockSpec((SWA_KV_HEADS, g_per * w), lambda b, t: (0, 0)),
            pl.BlockSpec((nq, SWA_SUB * w), lambda b, t: (0, b * n_t + t)),
            kv_spec(0), kv_spec(1), kv_spec(2),
        ],
        out_specs=pl.BlockSpec((SWA_SUB * w, nq), lambda b, t: (b * n_t + t, 0)),
        out_shape=jax.ShapeDtypeStruct((m, nq), BF16),
        scratch_shapes=[pltpu.VMEM((seq // w, 2 * LANES, w), BF16),
                        pltpu.VMEM((SWA_SUB * SWA_KV_HEADS, w, g_per * w), F32)],
        compiler_params=_cparams(2),
        name="swa",
    )(sink_rows, qt, ob, ob, ob)


MOBA_BIAS_ROWS = 16
MOBA_Q_SCALE = MOBA_HEAD_DIM ** -0.5 * math.log2(math.e)


def _moba_kernel(qt_ref, k_ref, vt, o_ref, kaug, kmean, qaug, s_a, s_b, *, seq, cb, qb):
    blk = MOBA_BLOCK
    dh = MOBA_HEAD_DIM
    nblk = seq // blk
    kc = cb * blk
    wq = qb * blk
    nbr = MOBA_BIAS_ROWS
    i = pl.program_id(2)

    first_call_step = (pl.program_id(0) == 0) & (pl.program_id(1) == 0) & (i == 0)

    @pl.when(first_call_step)
    def _():
        row_blk = lax.broadcasted_iota(jnp.int32, (seq, LANES), 0) // blk
        lane = lax.broadcasted_iota(jnp.int32, (seq, LANES), 1)
        kaug[:, dh:] = (row_blk == lane).astype(BF16)
        qaug[dh + nbr:, :] = jnp.zeros((dh - nbr, wq), BF16)
        kmean[...] = jnp.zeros_like(kmean)

    @pl.when(i == 0)
    def _():
        k = k_ref[...]
        kaug[:, :dh] = k
        kmean[:nblk, :] = jnp.mean(k.astype(F32).reshape(nblk, blk, dh), axis=1)

    q_t = qt_ref[...]
    gate = jnp.dot(kmean[...].astype(BF16), q_t, preferred_element_type=F32)
    row = lax.broadcasted_iota(jnp.int32, (nbr, wq), 0)
    rowf = row.astype(F32)
    own = i * qb + lax.broadcasted_iota(jnp.int32, (nbr, wq), 1) // blk
    past = row < own
    g = jnp.where(past, gate, -jnp.inf)
    sel = row == own
    for _ in range(min(MOBA_TOPK, nblk - 1)):
        mx = jnp.max(g, axis=0, keepdims=True)
        idx = jnp.min(jnp.where(g == mx, rowf, float(nbr)), axis=0, keepdims=True)
        hit = rowf == idx
        sel = sel | (hit & past)
        g = jnp.where(hit, -jnp.inf, g)
    qaug[:dh, :] = q_t
    qaug[dh:dh + nbr, :] = jnp.where(sel, 0.0, NEG).astype(BF16)

    causal = (lax.broadcasted_iota(jnp.int32, (blk, blk), 0)
              <= lax.broadcasted_iota(jnp.int32, (blk, blk), 1))
    for case in range(nblk // cb):
        @pl.when(i == case)
        def _(case=case):
            order = [case] + list(range(case))
            s_bufs = (s_a, s_b)

            def stage_scores(idx):
                c = order[idx]
                s = jnp.dot(kaug[c * kc:(c + 1) * kc, :], qaug[...],
                            preferred_element_type=F32)
                s_bufs[idx % 2][...] = s
                if c == case:
                    for d in range(qb):
                        diag = slice(d * blk, (d + 1) * blk)
                        s_bufs[idx % 2][diag, diag] = jnp.where(causal, s[diag, diag], NEG)

            stage_scores(0)
            m_col = l_col = acc = None
            for idx, c in enumerate(order):
                if idx + 1 < len(order):
                    stage_scores(idx + 1)
                rows = slice(c * kc, (c + 1) * kc)
                s = s_bufs[idx % 2][...]
                m_c = jnp.max(s, axis=0, keepdims=True)
                if m_col is None:
                    m_col = m_c
                    p = jnp.exp2(s - m_col)
                    l_col = jnp.sum(p, axis=0, keepdims=True)
                    acc = jnp.dot(vt[:, rows], p.astype(BF16), preferred_element_type=F32)
                else:
                    m_new = jnp.maximum(m_col, m_c)
                    alpha = jnp.exp2(m_col - m_new)
                    p = jnp.exp2(s - m_new)
                    l_col = alpha * l_col + jnp.sum(p, axis=0, keepdims=True)
                    acc = alpha * acc + jnp.dot(vt[:, rows], p.astype(BF16),
                                                preferred_element_type=F32)
                    m_col = m_new
            o_ref[...] = (acc / l_col).T.astype(BF16)


def _moba(qvt, ob, batch, seq, cb, qb):
    assert cb == qb
    m = batch * seq
    blk = MOBA_BLOCK
    dh = MOBA_HEAD_DIM
    nh = MOBA_HEADS
    n_q = seq // (qb * blk)
    wq = qb * blk
    return pl.pallas_call(
        functools.partial(_moba_kernel, seq=seq, cb=cb, qb=qb),
        grid=(batch, nh, n_q),
        in_specs=[
            pl.BlockSpec((dh, wq), lambda b, h, i: (h, b * n_q + i)),
            pl.BlockSpec((seq, dh), lambda b, h, i: (b, h)),
            pl.BlockSpec((dh, seq), lambda b, h, i: (nh + h, b)),
        ],
        out_specs=pl.BlockSpec((wq, dh), lambda b, h, i: (b * n_q + i, h)),
        out_shape=jax.ShapeDtypeStruct((m, nh * dh), BF16),
        scratch_shapes=[pltpu.VMEM((seq, 2 * dh), BF16),
                        pltpu.VMEM((MOBA_BIAS_ROWS, dh), F32),
                        pltpu.VMEM((2 * dh, wq), BF16),
                        pltpu.VMEM((cb * blk, wq), F32),
                        pltpu.VMEM((cb * blk, wq), F32)],
        compiler_params=_cparams(3),
        name="moba",
    )(qvt, ob, qvt)


def _ret_kernel(qk_ref, v_ref, dec_ref, xi_ref, zeta_ref, o_ref, r_scr, *, n_chunk, g_chunk):
    t = RET_CHUNK
    nh, dk, dv = RET_HEADS, RET_QK_DIM, RET_V_DIM

    @pl.when(pl.program_id(1) == 0)
    def _():
        r_scr[...] = jnp.zeros_like(r_scr)

    def body(c, carry):
        r0 = pl.multiple_of(c * t, t)
        for h in range(nh):
            q = qk_ref[pl.ds(r0, t), h * dk:(h + 1) * dk]
            k = qk_ref[pl.ds(r0, t), (nh + h) * dk:(nh + h + 1) * dk]
            v = v_ref[pl.ds(r0, t), h * dv:(h + 1) * dv]
            s = lax.dot_general(q, k, _NT, preferred_element_type=F32) * dec_ref[h]
            inner = jnp.dot(s.astype(BF16), v, preferred_element_type=F32)
            r_prev = r_scr[h]
            q_x = (q.astype(F32) * xi_ref[h]).astype(BF16)
            cross = jnp.dot(q_x, r_prev.astype(BF16), preferred_element_type=F32)
            o = inner + cross
            o = o * lax.rsqrt(jnp.mean(o * o, axis=-1, keepdims=True) + EPS)
            o_ref[pl.ds(r0, t), h * dv:(h + 1) * dv] = o.astype(BF16)
            k_z = (k.astype(F32) * zeta_ref[h]).astype(BF16)
            kv = lax.dot_general(k_z, v, _TN, preferred_element_type=F32)
            r_scr[h] = g_chunk[h] * r_prev + kv
        return carry

    lax.fori_loop(0, n_chunk, body, 0, unroll=True)


def _ret_constants():
    nh, t, dk = RET_HEADS, RET_CHUNK, RET_QK_DIM
    lin = [math.log(1.0 / 32) + (math.log(1.0 / 512) - math.log(1.0 / 32)) * h / (nh - 1)
           for h in range(nh)]
    log_g = [math.log1p(-math.exp(v)) for v in lin]
    i = jnp.arange(t, dtype=F32)
    lg = jnp.asarray(log_g, F32)
    diff = i[:, None] - i[None, :]
    decay = jnp.where(diff >= 0, jnp.exp(jnp.maximum(diff, 0.0)[None] * lg[:, None, None]), 0.0)
    xi = jnp.exp((i + 1)[None, :] * lg[:, None])
    zeta = jnp.exp((t - 1 - i)[None, :] * lg[:, None])
    xi_t = jnp.broadcast_to(xi[:, :, None], (nh, t, dk))
    zeta_t = jnp.broadcast_to(zeta[:, :, None], (nh, t, dk))
    g_chunk = tuple(math.exp(t * v) for v in log_g)
    return decay, xi_t, zeta_t, g_chunk


def _ret(ob, batch, seq, seg):
    m = batch * seq
    nh, dk, dv, t = RET_HEADS, RET_QK_DIM, RET_V_DIM, RET_CHUNK
    n_seg = seq // seg
    decay, xi_t, zeta_t, g_chunk = _ret_constants()
    wqk = 2 * nh * dk
    wv = nh * dv
    const = lambda b, s: (0, 0, 0)
    return pl.pallas_call(
        functools.partial(_ret_kernel, n_chunk=seg // t, g_chunk=g_chunk),
        grid=(batch, n_seg),
        in_specs=[
            pl.BlockSpec((seg, wqk), lambda b, s: (b * n_seg + s, 0)),
            pl.BlockSpec((seg, wv), lambda b, s: (b * n_seg + s, wqk // wv)),
            pl.BlockSpec((nh, t, t), const),
            pl.BlockSpec((nh, t, dk), const),
            pl.BlockSpec((nh, t, dk), const),
        ],
        out_specs=pl.BlockSpec((seg, wv), lambda b, s: (b * n_seg + s, 0)),
        out_shape=jax.ShapeDtypeStruct((m, wv), BF16),
        scratch_shapes=[pltpu.VMEM((nh, dk, dv), F32)],
        compiler_params=_cparams(2),
        name="retention",
    )(ob, ob, decay, xi_t, zeta_t)


def _out_kernel(mix_ref, qm_ref, z_ref, x_ref, mk_ref, mv_ref, w_hbm, fn_ref, o_ref, y_scr, w_ref,
                stage, sem, *, final):
    dm = MEM_HEAD_DIM
    scale = dm ** -0.5
    n_slices = BRANCH_WIDTH // COL_TILE
    rows = stage.shape[1]
    per = COL_TILE // rows
    start, fetch_piece = _weight_stream(
        w_hbm, w_ref, stage, sem,
        [slice(j * rows, (j + 1) * rows) for j in range(BRANCH_WIDTH // rows)], axis=0)

    def fetch(t):
        for j in range(per):
            fetch_piece(t * per + j)

    def body(first_step):
        if first_step:
            start()
        out = x_ref[...]
        for t in range(MIX_WIDTH // COL_TILE):
            lo, hi = t * COL_TILE, (t + 1) * COL_TILE
            y_t = (mix_ref[:, lo:hi].astype(F32) * _silu(z_ref[:, lo:hi])).astype(BF16)
            if first_step:
                fetch(t)
            out = out + jnp.dot(y_t, w_ref[lo:hi, :], preferred_element_type=F32)
        for h in range(MEM_HEADS):
            lo, hi = h * dm, (h + 1) * dm
            s = lax.dot_general(qm_ref[:, lo:hi], mk_ref[:, lo:hi], _NT,
                                preferred_element_type=F32) * scale
            p = jnp.exp(s - jnp.max(s, axis=-1, keepdims=True))
            l = jnp.sum(p, axis=-1, keepdims=True)
            o = jnp.dot(p.astype(BF16), mv_ref[:, lo:hi], preferred_element_type=F32) / l
            y_scr[:, lo:hi] = (o * _silu(z_ref[:, MIX_WIDTH + lo:MIX_WIDTH + hi])).astype(BF16)
        if first_step:
            fetch(n_slices - 1)
        out = out + jnp.dot(y_scr[...], w_ref[MIX_WIDTH:, :], preferred_element_type=F32)
        if final:
            ms = jnp.mean(out * out, axis=-1, keepdims=True)
            out = (out * lax.rsqrt(ms + EPS)) * fn_ref[...]
        o_ref[...] = out

    @pl.when(pl.program_id(0) == 0)
    def _():
        body(True)

    @pl.when(pl.program_id(0) != 0)
    def _():
        body(False)


def _out_proj(mix, ob, qm_blk, z, x2d, mkv, w_out, final_norm, batch, seq, tm, final):
    m, d = x2d.shape
    per_b = seq // tm
    row = lambda i: (i, 0)
    return pl.pallas_call(
        functools.partial(_out_kernel, final=final),
        grid=(m // tm,),
        in_specs=[
            pl.BlockSpec((tm, MIX_WIDTH), row),
            pl.BlockSpec((tm, MEM_WIDTH), lambda i: (i, qm_blk)),
            pl.BlockSpec((tm, BRANCH_WIDTH), row),
            pl.BlockSpec((tm, d), row),
            pl.BlockSpec((N_MEM, MEM_WIDTH), lambda i: (i // per_b, 0)),
            pl.BlockSpec((N_MEM, MEM_WIDTH), lambda i: (i // per_b, 1)),
            pl.BlockSpec(memory_space=pl.ANY),
            pl.BlockSpec((1, d), lambda i: (0, 0)),
        ],
        out_specs=pl.BlockSpec((tm, d), row),
        out_shape=jax.ShapeDtypeStruct((m, d), F32),
        scratch_shapes=[pltpu.VMEM((tm, MEM_WIDTH), BF16),
                        pltpu.VMEM((BRANCH_WIDTH, d), BF16),
                        pltpu.VMEM((2, COL_TILE // 2, d), F32),
                        pltpu.SemaphoreType.DMA((2,))],
        compiler_params=_cparams(1),
        name="out_proj",
    )(mix, ob, z, x2d, mkv, mkv, w_out, final_norm.reshape(1, d))


def _in_proj_plan(mixer):
    n_qm = MEM_WIDTH // LANES
    n_z = BRANCH_WIDTH // LANES
    if mixer == 0:
        n_q = SWA_Q_HEADS * SWA_HEAD_DIM // LANES
        plan = [("RST", gq) for gq in range(n_q)]
        plan += [("R", n_qm), ("Rh", n_qm + 1), ("P", n_qm + 2)]
        plan += [("P", j) for j in range(n_qm)]
    elif mixer == 1:
        n_h = MIX_WIDTH // LANES
        plan = [("RST", j) for j in range(n_h)] + [("R", j) for j in range(n_h)]
        plan += [("PT", n_h + j) for j in range(n_h)] + [("P", n_h + j) for j in range(n_qm)]
    else:
        n_qk = RET_HEADS * RET_QK_DIM // LANES
        kinds = ["R"] * n_qk + ["RS"] * n_qk + ["P"] * (MIX_WIDTH // LANES + n_qm)
        plan = [(kind, j) for j, kind in enumerate(kinds)]
    return plan + [("P", None)] * n_z


def kernel(x, mem, positions, mem_norm, w_mem_kv, norm_0, w_in_0, sinks_0, w_out_0, norm_1, w_in_1,
           w_out_1, norm_2, w_in_2, w_out_2, norm_3, w_in_3, sinks_3, w_out_3, final_norm):
    batch, seq, d = x.shape
    m = batch * seq
    layers = [(norm_0, w_in_0, w_out_0, sinks_0), (norm_1, w_in_1, w_out_1, None),
              (norm_2, w_in_2, w_out_2, None), (norm_3, w_in_3, w_out_3, sinks_3)]

    mkv = _mem_kv(mem.reshape(batch * N_MEM, d), mem_norm, w_mem_kv.astype(BF16))

    pos_col = positions.astype(F32).reshape(m, 1)
    rope = (
        (_rope_lane_rows(SWA_HEAD_DIM, SWA_HEAD_DIM // ROPE_FRACTION, ROPE_THETA),
         SWA_HEAD_DIM // ROPE_FRACTION // 2),
        (_rope_lane_rows(MOBA_HEAD_DIM, MOBA_HEAD_DIM // ROPE_FRACTION, ROPE_THETA),
         MOBA_HEAD_DIM // ROPE_FRACTION // 2),
        (_rope_lane_rows(RET_QK_DIM, RET_QK_DIM, RET_THETA), RET_QK_DIM // 2),
    )

    tiles = _tile_sizes(m, seq)
    h = x.reshape(m, d)
    n_layers = len(layers)
    for li, (g, w_in, w_out, sinks) in enumerate(layers):
        mixer = li % N_MIXERS
        lane_rows, half = rope[mixer]
        rs_scale = (SWA_Q_SCALE, MOBA_Q_SCALE, RET_QK_DIM ** -0.5)[mixer]
        ob, z, *qt = _in_proj(h, g, w_in, pos_col, lane_rows, _in_proj_plan(mixer), half,
                              rs_scale, tiles["in_rows"][mixer])
        if mixer == 0:
            mix = _swa(qt[0], ob, sinks, batch, seq)
            qm_blk = 0
        elif mixer == 1:
            mix = _moba(qt[0], ob, batch, seq, cb=tiles["moba_blocks"], qb=tiles["moba_blocks"])
            qm_blk = MIX_WIDTH // MEM_WIDTH
        else:
            mix = _ret(ob, batch, seq, seg=tiles["ret_rows"])
            qm_blk = (2 * RET_HEADS * RET_QK_DIM + MIX_WIDTH) // MEM_WIDTH
        h = _out_proj(mix, ob, qm_blk, z, h, mkv, w_out, final_norm, batch, seq,
                      tiles["out_rows"], final=(li == n_layers - 1))
    return h.reshape(batch, seq, d)
```
